```python
import jax, jax.numpy as jnp
from jax import lax
import numpy as np

D_MODEL = 1024
BATCH = 8
SEQ = 2048
DEPTH = 4

GRID_W = 64
CTX_LEN = 256
N_MIXERS = 3
EPS = 1e-6
CHUNK = 128
A_WIDTH = D_MODEL
A_GROUPS = 8
A_GROUP_DIM = A_WIDTH // A_GROUPS
HEAD_DIM = 128
N_HEADS = D_MODEL // HEAD_DIM
N_KV_HEADS = 2
Q_PER_KV = N_HEADS // N_KV_HEADS
Q_BLOCK = 128
ROPE_THETA = 10000.0
POOL_WINDOWS = (2, 4, 8, 16)
POOL_GROUPS = len(POOL_WINDOWS)
POOL_GROUP_DIM = D_MODEL // POOL_GROUPS
N_EXPERTS = 16
EXPERT_DIM = 1024
CAPACITY_FACTOR = 2

kernel_name = "hybrid_interleaved_gmlp_gqa_pool_ecmoe_dit"


def rms_norm(x, g):
    xf = x.astype(jnp.float32)
    y = xf * lax.rsqrt(jnp.mean(xf * xf, axis=-1, keepdims=True) + EPS)
    return (y * g.astype(jnp.float32)).astype(x.dtype)


def layer_norm(x, g):
    xf = x.astype(jnp.float32)
    mu = jnp.mean(xf, axis=-1, keepdims=True)
    xc = xf - mu
    y = xc * lax.rsqrt(jnp.mean(xc * xc, axis=-1, keepdims=True) + EPS)
    return (y * g.astype(jnp.float32)).astype(x.dtype)


def modulate(h, shift, scale):
    return h * (1 + scale) + shift


def chunk_gmlp(h, w_in, v_g, s_w, s_b, w_out):
    b, n, _ = h.shape
    z = jax.nn.gelu(h @ w_in)
    u, v = jnp.split(z, 2, axis=-1)
    v = layer_norm(v, v_g)
    v = v.reshape(b, n // CHUNK, CHUNK, A_GROUPS, A_GROUP_DIM)
    v = jnp.einsum('gpq,bcqgd->bcpgd', s_w, v) + s_b.T[:, :, None]
    return (u * v.reshape(b, n, A_WIDTH)) @ w_out


def axial_rope(n):
    rows = n // GRID_W
    row = jnp.repeat(jnp.arange(rows, dtype=jnp.float32), GRID_W)
    col = jnp.tile(jnp.arange(GRID_W, dtype=jnp.float32), rows)
    n_freq = HEAD_DIM // 4
    inv = jnp.power(ROPE_THETA, -jnp.arange(n_freq, dtype=jnp.float32) / n_freq)
    ang = jnp.concatenate([row[:, None] * inv, col[:, None] * inv], axis=-1)
    return jnp.cos(ang), jnp.sin(ang)


def apply_rope(x, cos, sin):
    x1, x2 = jnp.split(x.astype(jnp.float32), 2, axis=-1)
    cos = cos[None, :, None, :]
    sin = sin[None, :, None, :]
    return jnp.concatenate([x1 * cos - x2 * sin, x1 * sin + x2 * cos], axis=-1).astype(x.dtype)


def attend(qb, k, v):
    s = jnp.einsum('bqkgd,bskd->bkgqs', qb, k).astype(jnp.float32) * (HEAD_DIM ** -0.5)
    p = jax.nn.softmax(s, axis=-1).astype(v.dtype)
    return jnp.einsum('bkgqs,bskd->bqkgd', p, v)


def gqa_attention(h, hc, w_in, q_g, k_g, w_out, with_ctx_queries):
    b, n, _ = h.shape
    qd = N_HEADS * HEAD_DIM
    kvd = N_KV_HEADS * HEAD_DIM

    def proj_q(t):
        return rms_norm((t @ w_in[:, :qd]).reshape(b, -1, N_HEADS, HEAD_DIM), q_g)

    def proj_kv(t):
        kv = t @ w_in[:, qd:]
        k = rms_norm(kv[..., :kvd].reshape(b, -1, N_KV_HEADS, HEAD_DIM), k_g)
        return k, kv[..., kvd:].reshape(b, -1, N_KV_HEADS, HEAD_DIM)

    cos, sin = axial_rope(n)
    q = apply_rope(proj_q(h), cos, sin)
    k, v = proj_kv(h)
    k = apply_rope(k, cos, sin)
    kc, vc = proj_kv(hc)
    k_all = jnp.concatenate([kc, k], axis=1)
    v_all = jnp.concatenate([vc, v], axis=1)
    qb = q.reshape(b, n // Q_BLOCK, Q_BLOCK, N_KV_HEADS, Q_PER_KV, HEAD_DIM).transpose(1, 0, 2, 3, 4, 5)
    o = lax.map(lambda blk: attend(blk, k_all, v_all), qb)
    y = o.transpose(1, 0, 2, 3, 4, 5).reshape(b, n, qd) @ w_out
    yc = None
    if with_ctx_queries:
        m = hc.shape[1]
        qc = proj_q(hc).reshape(b, m, N_KV_HEADS, Q_PER_KV, HEAD_DIM)
        yc = attend(qc, kc, vc).reshape(b, m, qd) @ w_out
    return y, yc


def centred_window_mean(x, w):
    b, n, d = x.shape
    half = w // 2
    xp = jnp.pad(x.astype(jnp.float32), ((0, 0), (half, half), (0, 0)))
    cs = jnp.concatenate([jnp.zeros((b, 1, d), jnp.float32), jnp.cumsum(xp, axis=1)], axis=1)
    s = cs[:, w:w + n] - cs[:, :n]
    t = jnp.arange(n)
    cnt = (jnp.minimum(t + half - 1, n - 1) - jnp.maximum(t - half, 0) + 1).astype(jnp.float32)
    return (s / cnt[None, :, None]).astype(x.dtype)


def multiscale_pool(h, w_in, w_grp, scale):
    b, n, _ = h.shape
    z = (h @ w_in).reshape(b, n, POOL_GROUPS, POOL_GROUP_DIM)
    pooled = jnp.stack([centred_window_mean(z[:, :, gi], w) - z[:, :, gi]
                        for gi, w in enumerate(POOL_WINDOWS)], axis=2)
    y = jnp.einsum('bngd,gde->bnge', pooled, w_grp).reshape(b, n, D_MODEL)
    return y * scale


def expert_choice_ffn(h, router_w, w_gate, w_up, w_down):
    b, n, d = h.shape
    cap = CAPACITY_FACTOR * n // N_EXPERTS
    aff = jax.nn.softmax((h @ router_w).astype(jnp.float32), axis=-1)
    g, idx = lax.top_k(jnp.swapaxes(aff, 1, 2), cap)
    xin = jax.vmap(lambda hb, ib: hb[ib])(h, idx)
    hid = jax.nn.silu(jnp.einsum('becd,edf->becf', xin, w_gate)) * jnp.einsum('becd,edf->becf', xin, w_up)
    y = jnp.einsum('becf,efd->becd', hid, w_down) * g[..., None].astype(h.dtype)
    return jax.vmap(lambda yb, ib: jax.ops.segment_sum(yb.reshape(-1, d), ib.reshape(-1), num_segments=n))(y, idx)


def setup_inputs(seed: int = 0) -> dict:
    key = jax.random.key(seed)
    ks = iter(jax.random.split(key, 32))
    D = D_MODEL
    n_a = (DEPTH + 2) // 3
    n_b = (DEPTH + 1) // 3
    n_c = DEPTH // 3

    def nrm(shape, scale):
        return jax.random.normal(next(ks), shape, jnp.float32) * scale

    def gain(shape):
        return 1.0 + nrm(shape, 0.1)

    return {
        "x": nrm((BATCH, SEQ, D), 1.0),
        "c": nrm((BATCH, D), 1.0),
        "ctx": nrm((BATCH, CTX_LEN, D), 1.0),
        "c_ctx": nrm((D,), 1.0),
        "mod_w": nrm((DEPTH, D, 6 * D), D ** -0.5),
        "mod_b": nrm((DEPTH, 6 * D), 0.02),
        "norm1_g": gain((DEPTH, D)),
        "norm2_g": gain((DEPTH, D)),
        "router_w": nrm((DEPTH, D, N_EXPERTS), D ** -0.5),
        "exp_w_gate": nrm((DEPTH, N_EXPERTS, D, EXPERT_DIM), D ** -0.5),
        "exp_w_up": nrm((DEPTH, N_EXPERTS, D, EXPERT_DIM), D ** -0.5),
        "exp_w_down": nrm((DEPTH, N_EXPERTS, EXPERT_DIM, D), EXPERT_DIM ** -0.5),
        "g_w_in": nrm((n_a, D, 2 * A_WIDTH), D ** -0.5),
        "g_v_norm_g": gain((n_a, A_WIDTH)),
        "g_spatial_w": nrm((n_a, A_GROUPS, CHUNK, CHUNK), CHUNK ** -0.5),
        "g_spatial_b": gain((n_a, A_GROUPS, CHUNK)),
        "g_w_out": nrm((n_a, A_WIDTH, D), A_WIDTH ** -0.5),
        "att_w_in": nrm((n_b, D, (N_HEADS + 2 * N_KV_HEADS) * HEAD_DIM), D ** -0.5),
        "att_q_norm_g": gain((n_b, HEAD_DIM)),
        "att_k_norm_g": gain((n_b, HEAD_DIM)),
        "att_w_out": nrm((n_b, N_HEADS * HEAD_DIM, D), (N_HEADS * HEAD_DIM) ** -0.5),
        "p_w_in": nrm((n_c, D, D), D ** -0.5),
        "p_w_group": nrm((n_c, POOL_GROUPS, POOL_GROUP_DIM, POOL_GROUP_DIM), POOL_GROUP_DIM ** -0.5),
        "p_scale": gain((n_c, D)),
    }


def reference(x, c, ctx, c_ctx, mod_w, mod_b, norm1_g, norm2_g, router_w, exp_w_gate, exp_w_up,
              exp_w_down, g_w_in, g_v_norm_g, g_spatial_w, g_spatial_b, g_w_out, att_w_in,
              att_q_norm_g, att_k_norm_g, att_w_out, p_w_in, p_w_group, p_scale):
    for i in range(DEPTH):
        kind = i % N_MIXERS
        j = i // N_MIXERS
        update_ctx = any(l % N_MIXERS == 1 for l in range(i + 1, DEPTH))
        read_ctx = update_ctx or kind == 1

        m_lat = jax.nn.silu(c) @ mod_w[i] + mod_b[i]
        sh1, sc1, gt1, sh2, sc2, gt2 = jnp.split(m_lat[:, None, :], 6, axis=-1)
        h = modulate(rms_norm(x, norm1_g[i]), sh1, sc1)
        hc = None
        if read_ctx:
            m_ctx = jax.nn.silu(c_ctx) @ mod_w[i] + mod_b[i]
            csh1, csc1, cgt1, csh2, csc2, cgt2 = jnp.split(m_ctx, 6)
            hc = modulate(rms_norm(ctx, norm1_g[i]), csh1, csc1)

        yc = None
        if kind == 0:
            y = chunk_gmlp(h, g_w_in[j], g_v_norm_g[j], g_spatial_w[j], g_spatial_b[j], g_w_out[j])
            if update_ctx:
                yc = chunk_gmlp(hc, g_w_in[j], g_v_norm_g[j], g_spatial_w[j], g_spatial_b[j], g_w_out[j])
        elif kind == 1:
            y, yc = gqa_attention(h, hc, att_w_in[j], att_q_norm_g[j], att_k_norm_g[j], att_w_out[j], update_ctx)
        else:
            y = multiscale_pool(h, p_w_in[j], p_w_group[j], p_scale[j])
            if update_ctx:
                yc = multiscale_pool(hc, p_w_in[j], p_w_group[j], p_scale[j])

        x = x + gt1 * y
        h = modulate(rms_norm(x, norm2_g[i]), sh2, sc2)
        x = x + gt2 * expert_choice_ffn(h, router_w[i], exp_w_gate[i], exp_w_up[i], exp_w_down[i])
        if update_ctx:
            ctx = ctx + cgt1 * yc
            hc = modulate(rms_norm(ctx, norm2_g[i]), csh2, csc2)
            ctx = ctx + cgt2 * expert_choice_ffn(hc, router_w[i], exp_w_gate[i], exp_w_up[i], exp_w_down[i])
    return x
```

```python
import functools
import math

import jax
import jax.numpy as jnp
from jax import lax
from jax.experimental import pallas as pl
from jax.experimental.pallas import tpu as pltpu

F32 = jnp.float32
BF16 = jnp.bfloat16

D_MODEL = 1024
LANES = 128
SUBLANES = 8
N_CHUNK = D_MODEL // LANES
ROW_PAD = 8
EPS = 1e-6
GRID_W = 64
CHUNK = 128
A_GROUPS = 8
HEAD_DIM = 128
N_HEADS = 8
N_KV_HEADS = 2
Q_PER_KV = N_HEADS // N_KV_HEADS
ROPE_THETA = 10000.0
POOL_WINDOWS = (2, 4, 8, 16)
POOL_GROUP_DIM = D_MODEL // len(POOL_WINDOWS)
POOL_HALO = 8
N_EXPERTS = 16
CAPACITY_FACTOR = 2
VMEM_LIMIT = 56 * 1024 * 1024


def _cparams(*sem):
    return pltpu.CompilerParams(dimension_semantics=sem, vmem_limit_bytes=VMEM_LIMIT)


def _dot(a, b):
    return jnp.dot(a, b, preferred_element_type=F32)


def _dot_nt(a, b):
    return lax.dot_general(a, b, (((1,), (1,)), ((), ())), preferred_element_type=F32)


def _chunks_to_tile(ref, lead, rows):
    return jnp.concatenate([ref[lead + (j, rows)] for j in range(N_CHUNK)], axis=-1)


def _tile_to_chunks(ref, lead, rows, val):
    for j in range(N_CHUNK):
        ref[lead + (j, rows)] = val[:, j * LANES:(j + 1) * LANES]


def _rms_mod(x, g, shift, scale):
    y = x * lax.rsqrt(jnp.mean(x * x, axis=-1, keepdims=True) + EPS)
    return (y * g) * (1.0 + scale) + shift


def _mod_kernel(c_ref, w_ref, b_ref, o_ref):
    a = jax.nn.silu(c_ref[...]).astype(BF16)
    o_ref[0] = _dot(a, w_ref[0].astype(BF16)) + b_ref[0]


def _modulation(cc, mod_w, mod_b):
    depth, d, n = mod_w.shape
    tn = 1536
    return pl.pallas_call(
        _mod_kernel,
        grid=(depth, n // tn),
        in_specs=[
            pl.BlockSpec((16, d), lambda l, j: (0, 0)),
            pl.BlockSpec((1, d, tn), lambda l, j: (l, 0, j)),
            pl.BlockSpec((1, 1, tn), lambda l, j: (l, 0, j)),
        ],
        out_specs=pl.BlockSpec((1, 16, tn), lambda l, j: (l, 0, j)),
        out_shape=jax.ShapeDtypeStruct((depth, 16, n), F32),
        compiler_params=_cparams("parallel", "parallel"),
        name="adaln_modulation",
    )(cc, mod_w, mod_b.reshape(depth, 1, n))


def _load_x(x_ref, x_cm):
    if x_cm:
        return _chunks_to_tile(x_ref, (0,), slice(None))
    return x_ref[0]


def _epilogue(x, y, mod_ref, g2_ref, rw_ref, x1_ref, h2_ref, lg_ref):
    x1 = x + mod_ref[0, 2:3, :] * y
    _tile_to_chunks(x1_ref, (0,), slice(None), x1)
    h2 = _rms_mod(x1, g2_ref[...], mod_ref[0, 3:4, :], mod_ref[0, 4:5, :])
    _tile_to_chunks(h2_ref, (0,), slice(None), h2)
    hi = h2.astype(BF16)
    lo = (h2 - hi.astype(F32)).astype(BF16)
    rw = rw_ref[...]
    a = _dot_nt(rw, hi)
    b = _dot_nt(rw[:N_EXPERTS], lo)
    lg_ref[0] = a[:N_EXPERTS] + a[N_EXPERTS:] + b


def _x_spec(x_cm, tm, d):
    if x_cm:
        return pl.BlockSpec((1, N_CHUNK, tm, LANES), lambda b, i: (b, 0, i, 0))
    return pl.BlockSpec((1, tm, d), lambda b, i: (b, i, 0))


def _const_spec(shape):
    nd = len(shape)
    return pl.BlockSpec(shape, lambda b, i: (0,) * nd)


def _epilogue_specs(bsz, s, tm):
    sp = s + ROW_PAD
    cm_spec = pl.BlockSpec((1, N_CHUNK, tm, LANES), lambda b, i: (b, 0, i, 0))
    out_specs = [cm_spec, cm_spec, pl.BlockSpec((1, N_EXPERTS, tm), lambda b, i: (b, 0, i))]
    out_shape = [
        jax.ShapeDtypeStruct((bsz, N_CHUNK, sp, LANES), F32),
        jax.ShapeDtypeStruct((bsz, N_CHUNK, sp, LANES), F32),
        jax.ShapeDtypeStruct((bsz, N_EXPERTS, s), F32),
    ]
    return out_specs, out_shape


def _mod_spec():
    return pl.BlockSpec((1, 6, D_MODEL), lambda b, i: (b, 0, 0))


def _gmlp_kernel(x_ref, mod_ref, g1_ref, win_ref, vg_ref, sw_ref, sb_ref, wout_ref, g2_ref, rw_ref,
                 x1_ref, h2_ref, lg_ref, *, x_cm, tm):
    x = _load_x(x_ref, x_cm)
    h = _rms_mod(x, g1_ref[...], mod_ref[0, 0:1, :], mod_ref[0, 1:2, :]).astype(BF16)
    z = jax.nn.gelu(_dot(h, win_ref[...]))
    u = z[:, :D_MODEL]
    v = z[:, D_MODEL:]
    mu = jnp.mean(v, axis=-1, keepdims=True)
    vc = v - mu
    v = (vc * lax.rsqrt(jnp.mean(vc * vc, axis=-1, keepdims=True) + EPS)) * vg_ref[...]
    vb = v.astype(BF16)
    nc = tm // CHUNK
    mixed = []
    for g in range(A_GROUPS):
        cols = jnp.concatenate(
            [vb[c * CHUNK:(c + 1) * CHUNK, g * LANES:(g + 1) * LANES] for c in range(nc)], axis=1)
        mixed.append(_dot(sw_ref[g], cols))
    sv = jnp.concatenate(
        [jnp.concatenate([mixed[g][:, c * LANES:(c + 1) * LANES] for g in range(A_GROUPS)], axis=1)
         + sb_ref[...] for c in range(nc)], axis=0)
    y = _dot((u * sv).astype(BF16), wout_ref[...])
    _epilogue(x, y, mod_ref, g2_ref, rw_ref, x1_ref, h2_ref, lg_ref)


def _gmlp_layer(x, x_cm, s, mod, g1, w_in, v_g, s_w, s_bias, w_out, g2, rw2):
    bsz = x.shape[0]
    tm = min(512, s)
    out_specs, out_shape = _epilogue_specs(bsz, s, tm)
    return pl.pallas_call(
        functools.partial(_gmlp_kernel, x_cm=x_cm, tm=tm),
        grid=(bsz, s // tm),
        in_specs=[
            _x_spec(x_cm, tm, D_MODEL), _mod_spec(), _const_spec((1, D_MODEL)),
            _const_spec(w_in.shape), _const_spec((1, D_MODEL)), _const_spec(s_w.shape),
            _const_spec(s_bias.shape), _const_spec(w_out.shape), _const_spec((1, D_MODEL)),
            _const_spec(rw2.shape),
        ],
        out_specs=out_specs,
        out_shape=out_shape,
        compiler_params=_cparams("parallel", "arbitrary"),
        name="gmlp_mixer",
    )(x, mod, g1, w_in, v_g, s_w, s_bias, w_out, g2, rw2)


def _qkv_kernel(x_ref, mod_ref, g1_ref, w_ref, qg_ref, kg_ref, cos_ref, sin_ref, *out_refs,
                x_cm, want_q):
    x = _load_x(x_ref, x_cm)
    h = _rms_mod(x, g1_ref[...], mod_ref[0, 0:1, :], mod_ref[0, 1:2, :]).astype(BF16)
    qkv = _dot(h, w_ref[...])
    cos = cos_ref[...]
    sin = sin_ref[...]

    def norm_rope(t, g):
        t = (t * lax.rsqrt(jnp.mean(t * t, axis=-1, keepdims=True) + EPS)) * g
        return t * cos + pltpu.roll(t, HEAD_DIM // 2, 1) * sin

    qd = N_HEADS * HEAD_DIM
    kvd = N_KV_HEADS * HEAD_DIM
    if want_q:
        q_ref, k_ref, v_ref = out_refs
        for hd in range(N_HEADS):
            qh = norm_rope(qkv[:, hd * HEAD_DIM:(hd + 1) * HEAD_DIM], qg_ref[...])
            q_ref[0, :, hd * HEAD_DIM:(hd + 1) * HEAD_DIM] = (qh * (HEAD_DIM ** -0.5)).astype(BF16)
    else:
        k_ref, v_ref = out_refs
    for hd in range(N_KV_HEADS):
        kh = norm_rope(qkv[:, qd + hd * HEAD_DIM:qd + (hd + 1) * HEAD_DIM], kg_ref[...])
        k_ref[0, :, hd * HEAD_DIM:(hd + 1) * HEAD_DIM] = kh.astype(BF16)
    v_ref[0] = qkv[:, qd + kvd:].astype(BF16)


def _qkv_proj(x, x_cm, s, mod, g1, w_in, q_g, k_g, cos, sin, want_q):
    bsz = x.shape[0]
    tm = min(512, s)
    qd = N_HEADS * HEAD_DIM
    kvd = N_KV_HEADS * HEAD_DIM
    kv_spec = pl.BlockSpec((1, tm, kvd), lambda b, i: (b, i, 0))
    kv_shape = jax.ShapeDtypeStruct((bsz, s, kvd), BF16)
    out_specs = [kv_spec, kv_spec]
    out_shape = [kv_shape, kv_shape]
    if want_q:
        out_specs = [pl.BlockSpec((1, tm, qd), lambda b, i: (b, i, 0))] + out_specs
        out_shape = [jax.ShapeDtypeStruct((bsz, s, qd), BF16)] + out_shape
    return pl.pallas_call(
        functools.partial(_qkv_kernel, x_cm=x_cm, want_q=want_q),
        grid=(bsz, s // tm),
        in_specs=[
            _x_spec(x_cm, tm, D_MODEL), _mod_spec(), _const_spec((1, D_MODEL)),
            _const_spec(w_in.shape), _const_spec((1, HEAD_DIM)), _const_spec((1, HEAD_DIM)),
            pl.BlockSpec((tm, HEAD_DIM), lambda b, i: (i, 0)),
            pl.BlockSpec((tm, HEAD_DIM), lambda b, i: (i, 0)),
        ],
        out_specs=out_specs,
        out_shape=out_shape,
        compiler_params=_cparams("parallel", "arbitrary"),
        name="qkv_proj",
    )(x, mod, g1, w_in, q_g, k_g, cos, sin)


def _attn_kernel(x_ref, q_ref, k_ref, v_ref, mod_ref, wout_ref, g2_ref, rw_ref,
                 x1_ref, h2_ref, lg_ref, *, tq):
    x = _load_x(x_ref, True)
    heads = [None] * N_HEADS
    for kh in range(N_KV_HEADS):
        k = k_ref[0, :, kh * HEAD_DIM:(kh + 1) * HEAD_DIM]
        v = v_ref[0, :, kh * HEAD_DIM:(kh + 1) * HEAD_DIM]
        qg = jnp.concatenate(
            [q_ref[0, :, (kh * Q_PER_KV + g) * HEAD_DIM:(kh * Q_PER_KV + g + 1) * HEAD_DIM]
             for g in range(Q_PER_KV)], axis=0)
        sc = _dot_nt(qg, k)
        p = jnp.exp(sc - jnp.max(sc, axis=-1, keepdims=True))
        o = _dot(p.astype(BF16), v) / jnp.sum(p, axis=-1, keepdims=True)
        for g in range(Q_PER_KV):
            heads[kh * Q_PER_KV + g] = o[g * tq:(g + 1) * tq].astype(BF16)
    y = _dot(jnp.concatenate(heads, axis=1), wout_ref[...])
    _epilogue(x, y, mod_ref, g2_ref, rw_ref, x1_ref, h2_ref, lg_ref)


def _attn_layer(x, s, q, k_all, v_all, mod, w_out, g2, rw2):
    bsz = x.shape[0]
    tq = 128
    sk = k_all.shape[1]
    qd = N_HEADS * HEAD_DIM
    kvd = N_KV_HEADS * HEAD_DIM
    out_specs, out_shape = _epilogue_specs(bsz, s, tq)
    return pl.pallas_call(
        functools.partial(_attn_kernel, tq=tq),
        grid=(bsz, s // tq),
        in_specs=[
            _x_spec(True, tq, D_MODEL),
            pl.BlockSpec((1, tq, qd), lambda b, i: (b, i, 0)),
            pl.BlockSpec((1, sk, kvd), lambda b, i: (b, 0, 0)),
            pl.BlockSpec((1, sk, kvd), lambda b, i: (b, 0, 0)),
            _mod_spec(), _const_spec(w_out.shape), _const_spec((1, D_MODEL)), _const_spec(rw2.shape),
        ],
        out_specs=out_specs,
        out_shape=out_shape,
        compiler_params=_cparams("parallel", "arbitrary"),
        name="gqa_attention",
    )(x, q, k_all, v_all, mod, w_out, g2, rw2)


def _pool_kernel(xp_ref, x_ref, xn_ref, mod_ref, g1_ref, win_ref, wgrp_ref, ps_ref, g2_ref, rw_ref,
                 x1_ref, h2_ref, lg_ref, *, tm, s):
    i = pl.program_id(1)
    x = _load_x(x_ref, True)
    xe = jnp.concatenate([_load_x(xp_ref, True), x, _load_x(xn_ref, True)], axis=0)
    h = _rms_mod(xe, g1_ref[...], mod_ref[0, 0:1, :], mod_ref[0, 1:2, :]).astype(BF16)
    z = _dot(h, win_ref[...])
    te = tm + LANES
    zb = jnp.concatenate([z, jnp.zeros((te - tm - 2 * POOL_HALO, D_MODEL), F32)], axis=0).astype(BF16)
    pt = i * tm + lax.broadcasted_iota(jnp.int32, (tm, te), 0)
    col = lax.broadcasted_iota(jnp.int32, (tm, te), 1)
    ps = i * tm - POOL_HALO + col
    valid = (ps >= 0) & (ps < s) & (col < tm + 2 * POOL_HALO)
    tcol = i * tm + lax.broadcasted_iota(jnp.int32, (tm, 1), 0)
    outs = []
    for gi, w in enumerate(POOL_WINDOWS):
        half = w // 2
        band = jnp.where(valid & (ps >= pt - half) & (ps <= pt + half - 1), 1.0, 0.0).astype(BF16)
        cnt = (jnp.minimum(tcol + half - 1, s - 1) - jnp.maximum(tcol - half, 0) + 1).astype(F32)
        lo, hi_ = gi * POOL_GROUP_DIM, (gi + 1) * POOL_GROUP_DIM
        pooled = _dot(band, zb[:, lo:hi_]) / cnt - z[POOL_HALO:POOL_HALO + tm, lo:hi_]
        outs.append(_dot(pooled.astype(BF16), wgrp_ref[gi]))
    y = jnp.concatenate(outs, axis=1) * ps_ref[...]
    _epilogue(x, y, mod_ref, g2_ref, rw_ref, x1_ref, h2_ref, lg_ref)


def _pool_layer(x, s, mod, g1, w_in, w_grp, p_scale, g2, rw2):
    bsz = x.shape[0]
    tm = 256
    nb = tm // POOL_HALO
    last = s // POOL_HALO - 1
    out_specs, out_shape = _epilogue_specs(bsz, s, tm)
    halo = (1, N_CHUNK, POOL_HALO, LANES)
    return pl.pallas_call(
        functools.partial(_pool_kernel, tm=tm, s=s),
        grid=(bsz, s // tm),
        in_specs=[
            pl.BlockSpec(halo, lambda b, i: (b, 0, jnp.maximum(i * nb - 1, 0), 0)),
            _x_spec(True, tm, D_MODEL),
            pl.BlockSpec(halo, lambda b, i: (b, 0, jnp.minimum((i + 1) * nb, last), 0)),
            _mod_spec(), _const_spec((1, D_MODEL)), _const_spec(w_in.shape), _const_spec(w_grp.shape),
            _const_spec((1, D_MODEL)), _const_spec((1, D_MODEL)), _const_spec(rw2.shape),
        ],
        out_specs=out_specs,
        out_shape=out_shape,
        compiler_params=_cparams("parallel", "arbitrary"),
        name="pool_mixer",
    )(x, x, x, mod, g1, w_in, w_grp, p_scale, g2, rw2)


SORT_ROWS = 32


def _comes_first(a, ia, b, ib):
    return (a > b) | ((a == b) & (ia < ib))


def _bitonic_sort(keys, idxs, lane, n):
    nblk = len(keys)
    blk_bits = nblk.bit_length() - 1

    def forwards(c, k):
        if k < nblk:
            return (c & k) == 0
        if k >= n:
            return True
        return (lane & (k >> blk_bits)) == 0

    k = 2
    while k <= n:
        j = k // 2
        while j >= 1:
            if j < nblk:
                for c in range(nblk):
                    if c & j:
                        continue
                    p = c | j
                    fw = forwards(c, k)
                    keep = _comes_first(keys[c], idxs[c], keys[p], idxs[p])
                    if fw is False:
                        keep = jnp.logical_not(keep)
                    elif fw is not True:
                        keep = jnp.logical_xor(keep, jnp.logical_not(fw))
                    keys[c], keys[p] = jnp.where(keep, keys[c], keys[p]), jnp.where(keep, keys[p], keys[c])
                    idxs[c], idxs[p] = jnp.where(keep, idxs[c], idxs[p]), jnp.where(keep, idxs[p], idxs[c])
            else:
                jl = j >> blk_bits
                lower = (lane & jl) == 0
                for c in range(nblk):
                    fw = forwards(c, k)
                    later = jnp.logical_not(lower) if fw is True else jnp.logical_xor(lower, fw)
                    x, ix = keys[c], idxs[c]
                    px = jnp.where(lower, pltpu.roll(x, LANES - jl, 1), pltpu.roll(x, jl, 1))
                    pix = jnp.where(lower, pltpu.roll(ix, LANES - jl, 1), pltpu.roll(ix, jl, 1))
                    keep = jnp.logical_xor(_comes_first(x, ix, px, pix), later)
                    keys[c] = jnp.where(keep, x, px)
                    idxs[c] = jnp.where(keep, ix, pix)
            j //= 2
        k *= 2
    return keys, idxs


def _first_positions(blocks, lane, cap):
    nblk = len(blocks)
    per = cap // nblk
    out = []
    for v in range(-(-cap // LANES)):
        acc = None
        for c in range(v * (LANES // per), min(nblk, (v + 1) * (LANES // per))):
            off = (c * per) % LANES
            piece = blocks[c] if off == 0 else pltpu.roll(blocks[c], off, 1)
            acc = piece if acc is None else jnp.where(lane < off, acc, piece)
        out.append(acc)
    return out


def _route_kernel(lg_ref, idx_ref, gate_ref, aff_ref, *, s, cap):
    lg = lg_ref[...]
    ex = jnp.exp(lg - jnp.max(lg, axis=1, keepdims=True))
    aff = ex / jnp.sum(ex, axis=1, keepdims=True)
    n_rows = lg.shape[0] * N_EXPERTS
    aff_ref[...] = aff.reshape(n_rows, s)
    nblk = s // LANES
    grp = min(SORT_ROWS, n_rows) if nblk > 2 else n_rows
    lane = lax.broadcasted_iota(jnp.int32, (grp, LANES), 1)

    def sort_group(g, carry):
        rows = pl.ds(pl.multiple_of(g * grp, grp), grp)
        keys = [aff_ref[rows, c * LANES:(c + 1) * LANES] for c in range(nblk)]
        idxs = [lane + c * LANES for c in range(nblk)]
        keys, idxs = _bitonic_sort(keys, idxs, lane, s)
        top_keys = _first_positions(keys, lane, cap)
        top_idxs = _first_positions(idxs, lane, cap)
        for v in range(len(top_keys)):
            w = min(LANES, cap - v * LANES)
            gate_ref[rows, v * LANES:v * LANES + w] = top_keys[v][:, :w]
            idx_ref[rows, v * LANES:v * LANES + w] = top_idxs[v][:, :w]
        return carry

    lax.fori_loop(0, n_rows // grp, sort_group, 0)


def _route(logits_t, s):
    bsz = logits_t.shape[0]
    cap = CAPACITY_FACTOR * s // N_EXPERTS
    n_rows = bsz * N_EXPERTS
    return pl.pallas_call(
        functools.partial(_route_kernel, s=s, cap=cap),
        out_shape=[
            jax.ShapeDtypeStruct((n_rows, cap), jnp.int32),
            jax.ShapeDtypeStruct((n_rows, cap), F32),
        ],
        scratch_shapes=[pltpu.VMEM((n_rows, s), F32)],
        compiler_params=pltpu.CompilerParams(vmem_limit_bytes=VMEM_LIMIT),
        name="expert_choice_route",
    )(logits_t)


GATHER_UNROLL = 8


def _gather_kernel(idx_ref, src_ref, o_ref, *, sp, cap, s2, n_seg):
    src = src_ref.at[0]
    dst = o_ref.at[0]
    base = pl.program_id(1) * cap if n_seg > 1 else 0

    def body(i, carry):
        for u in range(GATHER_UNROLL):
            r = i * GATHER_UNROLL + u
            t = idx_ref[0, 0, r]
            dst[pl.ds(base + r, N_CHUNK, stride=s2), :] = src[pl.ds(t, N_CHUNK, stride=sp), :]
        return carry

    lax.fori_loop(0, cap // GATHER_UNROLL, body, 0)
    if n_seg == 1:
        for j in range(N_CHUNK):
            dst[pl.ds(j * s2 + cap, ROW_PAD), :] = jnp.zeros((ROW_PAD, LANES), F32)
    else:
        @pl.when(pl.program_id(1) == 0)
        def _():
            for j in range(N_CHUNK):
                dst[pl.ds(j * s2 + n_seg * cap, ROW_PAD), :] = jnp.zeros((ROW_PAD, LANES), F32)


def _gather_lat(idx, h2cm, s):
    bsz = h2cm.shape[0]
    sp = s + ROW_PAD
    cap = idx.shape[-1]
    s2 = cap + ROW_PAD
    return pl.pallas_call(
        functools.partial(_gather_kernel, sp=sp, cap=cap, s2=s2, n_seg=1),
        grid=(bsz, N_EXPERTS),
        in_specs=[
            pl.BlockSpec((1, 1, cap), lambda b, e: (b * N_EXPERTS + e, 0, 0), memory_space=pltpu.SMEM),
            pl.BlockSpec((1, N_CHUNK * sp, LANES), lambda b, e: (b, 0, 0)),
        ],
        out_specs=pl.BlockSpec((1, N_CHUNK * s2, LANES), lambda b, e: (e * bsz + b, 0, 0)),
        out_shape=jax.ShapeDtypeStruct((bsz * N_EXPERTS, N_CHUNK * s2, LANES), F32),
        compiler_params=_cparams("parallel", "arbitrary"),
        name="gather_rows",
    )(idx.reshape(bsz * N_EXPERTS, 1, cap), h2cm.reshape(bsz, N_CHUNK * sp, LANES))


def _gather_ctx(idx, h2cm, s):
    bsz = h2cm.shape[0]
    sp = s + ROW_PAD
    cap = idx.shape[-1]
    s2 = bsz * cap + ROW_PAD
    return pl.pallas_call(
        functools.partial(_gather_kernel, sp=sp, cap=cap, s2=s2, n_seg=bsz),
        grid=(N_EXPERTS, bsz),
        in_specs=[
            pl.BlockSpec((1, 1, cap), lambda e, b: (b * N_EXPERTS + e, 0, 0), memory_space=pltpu.SMEM),
            pl.BlockSpec((1, N_CHUNK * sp, LANES), lambda e, b: (b, 0, 0)),
        ],
        out_specs=pl.BlockSpec((1, N_CHUNK * s2, LANES), lambda e, b: (e, 0, 0)),
        out_shape=jax.ShapeDtypeStruct((N_EXPERTS, N_CHUNK * s2, LANES), F32),
        compiler_params=_cparams("parallel", "arbitrary"),
        name="gather_rows_ctx",
    )(idx.reshape(bsz * N_EXPERTS, 1, cap), h2cm.reshape(bsz, N_CHUNK * sp, LANES))


MOE_TILES = 2


def _ffn_tile(x_ref, o_ref, wg, wu, wd, n_tiles, rows, s2):
    x = jnp.concatenate(
        [jnp.concatenate([x_ref[n, pl.ds(j * s2, rows), :] for j in range(N_CHUNK)], axis=1)
         for n in range(n_tiles)], axis=0).astype(BF16)
    hid = (jax.nn.silu(_dot(x, wg[...])) * _dot(x, wu[...])).astype(BF16)
    y = _dot(hid, wd[...])
    for n in range(n_tiles):
        for j in range(N_CHUNK):
            o_ref[n, pl.ds(j * s2, rows), :] = y[n * rows:(n + 1) * rows, j * LANES:(j + 1) * LANES]
            o_ref[n, pl.ds(j * s2 + rows, ROW_PAD), :] = jnp.zeros((ROW_PAD, LANES), F32)


def _moe_kernel(*refs, n_lat, rows_lat, rows_ctx):
    if rows_ctx:
        xl_ref, xc_ref, wg_ref, wu_ref, wd_ref, ol_ref, oc_ref, wg, wu, wd = refs
    else:
        xl_ref, wg_ref, wu_ref, wd_ref, ol_ref, wg, wu, wd = refs
    b = pl.program_id(1)

    @pl.when(b == 0)
    def _():
        rb = 128

        def cast(i, carry):
            rows = pl.ds(pl.multiple_of(i * rb, rb), rb)
            wg[rows, :] = wg_ref[0, 0, rows, :].astype(BF16)
            wu[rows, :] = wu_ref[0, 0, rows, :].astype(BF16)
            wd[rows, :] = wd_ref[0, 0, rows, :].astype(BF16)
            return carry

        lax.fori_loop(0, D_MODEL // rb, cast, 0)

    if rows_ctx:
        @pl.when(b < n_lat)
        def _():
            _ffn_tile(xl_ref, ol_ref, wg, wu, wd, MOE_TILES, rows_lat, rows_lat + ROW_PAD)

        @pl.when(b == n_lat)
        def _():
            _ffn_tile(xc_ref, oc_ref, wg, wu, wd, 1, rows_ctx, rows_ctx + ROW_PAD)
    else:
        _ffn_tile(xl_ref, ol_ref, wg, wu, wd, MOE_TILES, rows_lat, rows_lat + ROW_PAD)


def _moe_ffn(xin_lat, xin_ctx, w_gate, w_up, w_down, layer, bsz):
    rows_lat = xin_lat.shape[1] // N_CHUNK - ROW_PAD
    rows_ctx = 0 if xin_ctx is None else xin_ctx.shape[1] // N_CHUNK - ROW_PAD
    n_lat = bsz // MOE_TILES
    n_steps = n_lat + (1 if rows_ctx else 0)
    lat_spec = pl.BlockSpec((MOE_TILES,) + xin_lat.shape[1:],
                            lambda e, b: (e * n_lat + jnp.minimum(b, n_lat - 1), 0, 0))
    w_spec = pl.BlockSpec((1, 1) + w_gate.shape[2:], lambda e, b: (layer, e, 0, 0))
    in_specs = [lat_spec]
    out_specs = [lat_spec]
    out_shape = [jax.ShapeDtypeStruct(xin_lat.shape, F32)]
    args = [xin_lat]
    if rows_ctx:
        ctx_spec = pl.BlockSpec((1,) + xin_ctx.shape[1:], lambda e, b: (e, 0, 0))
        in_specs.append(ctx_spec)
        out_specs.append(ctx_spec)
        out_shape.append(jax.ShapeDtypeStruct(xin_ctx.shape, F32))
        args.append(xin_ctx)
    in_specs += [w_spec, w_spec, w_spec]
    args += [w_gate, w_up, w_down]
    return pl.pallas_call(
        functools.partial(_moe_kernel, n_lat=n_lat, rows_lat=rows_lat, rows_ctx=rows_ctx),
        grid=(N_EXPERTS, n_steps),
        in_specs=in_specs,
        out_specs=out_specs,
        out_shape=out_shape,
        scratch_shapes=[pltpu.VMEM(w_gate.shape[2:], BF16), pltpu.VMEM(w_up.shape[2:], BF16),
                        pltpu.VMEM(w_down.shape[2:], BF16)],
        compiler_params=_cparams("arbitrary", "arbitrary"),
        name="expert_ffn",
    )(*args)


COMBINE_UNROLL = 8


def _combine_kernel(idx_ref, gate_ref, x_ref, y_ref, mod_ref, o_ref, *scratch, s, sp, cap, s2, n_seg, to_rows):
    e = pl.program_id(1)
    acc = scratch[0] if to_rows else o_ref.at[0]
    x = x_ref.at[0]
    y = y_ref.at[0]
    base = pl.program_id(0) * cap if n_seg > 1 else 0

    @pl.when(e == 0)
    def _():
        for j in range(N_CHUNK):
            acc[pl.ds(j * sp, s), :] = x[pl.ds(j * sp, s), :]
            acc[pl.ds(j * sp + s, ROW_PAD), :] = jnp.zeros((ROW_PAD, LANES), F32)

    g2 = mod_ref[0, 5]

    def body(i, carry):
        new = []
        for u in range(COMBINE_UNROLL):
            r = i * COMBINE_UNROLL + u
            t = idx_ref[0, 0, r]
            yrow = y[pl.ds(base + r, N_CHUNK, stride=s2), :]
            cur = acc[pl.ds(t, N_CHUNK, stride=sp), :]
            new.append((t, cur + (gate_ref[0, 0, r] * g2) * yrow))
        for t, val in new:
            acc[pl.ds(t, N_CHUNK, stride=sp), :] = val
        return carry

    lax.fori_loop(0, cap // COMBINE_UNROLL, body, 0)

    if to_rows:
        @pl.when(e == N_EXPERTS - 1)
        def _():
            for j in range(N_CHUNK):
                o_ref[0, :, j * LANES:(j + 1) * LANES] = acc[pl.ds(j * sp, s), :]


def _combine(idx, gate, x1cm, ycm, mod, s, ctx_tiles, to_rows):
    bsz = x1cm.shape[0]
    sp = s + ROW_PAD
    cap = idx.shape[-1]
    n_seg = bsz if ctx_tiles else 1
    s2 = n_seg * cap + ROW_PAD
    if ctx_tiles:
        y_spec = pl.BlockSpec((1,) + ycm.shape[1:], lambda b, e: (e, 0, 0))
    else:
        y_spec = pl.BlockSpec((1,) + ycm.shape[1:], lambda b, e: (e * bsz + b, 0, 0))
    smem = functools.partial(pl.BlockSpec, memory_space=pltpu.SMEM)
    if to_rows:
        out_spec = pl.BlockSpec((1, s, D_MODEL), lambda b, e: (b, 0, 0))
        out_shape = jax.ShapeDtypeStruct((bsz, s, D_MODEL), F32)
        scratch = [pltpu.VMEM((N_CHUNK * sp, LANES), F32)]
    else:
        out_spec = pl.BlockSpec((1, N_CHUNK * sp, LANES), lambda b, e: (b, 0, 0))
        out_shape = jax.ShapeDtypeStruct((bsz, N_CHUNK * sp, LANES), F32)
        scratch = []
    out = pl.pallas_call(
        functools.partial(_combine_kernel, s=s, sp=sp, cap=cap, s2=s2, n_seg=n_seg, to_rows=to_rows),
        grid=(bsz, N_EXPERTS),
        in_specs=[
            smem((1, 1, cap), lambda b, e: (b * N_EXPERTS + e, 0, 0)),
            smem((1, 1, cap), lambda b, e: (b * N_EXPERTS + e, 0, 0)),
            pl.BlockSpec((1, N_CHUNK * sp, LANES), lambda b, e: (b, 0, 0)),
            y_spec,
            pl.BlockSpec((1, 6, N_CHUNK, LANES), lambda b, e: (b, 0, 0, 0)),
        ],
        out_specs=out_spec,
        out_shape=out_shape,
        scratch_shapes=scratch,
        compiler_params=_cparams("parallel", "arbitrary"),
        name="combine_rows" if to_rows else "combine_cm",
    )(idx.reshape(bsz * N_EXPERTS, 1, cap), gate.reshape(bsz * N_EXPERTS, 1, cap),
      x1cm.reshape(bsz, N_CHUNK * sp, LANES), ycm, mod.reshape(bsz, 6, N_CHUNK, LANES))
    if to_rows:
        return out
    return out.reshape(bsz, N_CHUNK, sp, LANES)


def _rope_tables(n):
    rows = n // GRID_W
    row = jnp.repeat(jnp.arange(rows, dtype=F32), GRID_W)
    col = jnp.tile(jnp.arange(GRID_W, dtype=F32), rows)
    n_freq = HEAD_DIM // 4
    inv = jnp.power(ROPE_THETA, -jnp.arange(n_freq, dtype=F32) / n_freq)
    ang = jnp.concatenate([row[:, None] * inv, col[:, None] * inv], axis=-1)
    cos, sin = jnp.cos(ang), jnp.sin(ang)
    return jnp.concatenate([cos, cos], axis=-1), jnp.concatenate([-sin, sin], axis=-1)


def _router_rows(router_w):
    wt = router_w.T
    hi = wt.astype(BF16)
    lo = (wt - hi.astype(F32)).astype(BF16)
    return jnp.concatenate([hi, lo], axis=0)


def kernel(x, c, ctx, c_ctx, mod_w, mod_b, norm1_g, norm2_g, router_w, exp_w_gate, exp_w_up, exp_w_down,
           g_w_in, g_v_norm_g, g_spatial_w, g_spatial_b, g_w_out, att_w_in, att_q_norm_g, att_k_norm_g,
           att_w_out, p_w_in, p_w_group, p_scale):
    bsz, s, d = x.shape
    s_ctx = ctx.shape[1]
    depth = mod_w.shape[0]
    assert d == D_MODEL and bsz <= 8 and bsz % MOE_TILES == 0 and s % 512 == 0 and s_ctx % 128 == 0

    cc = jnp.zeros((16, d), F32).at[:bsz].set(c).at[bsz].set(c_ctx)
    mods = _modulation(cc, mod_w, mod_b).reshape(depth, 16, 6, d)

    x_cur, x_cm = x, False
    c_cur, c_cm = ctx, False
    for i in range(depth):
        kind, j = i % 3, i // 3
        update_ctx = any(l % 3 == 1 for l in range(i + 1, depth))
        last = i == depth - 1
        mod_lat = mods[i, :bsz]
        mod_ctx = jnp.broadcast_to(mods[i, bsz][None], (bsz, 6, d))
        g1 = norm1_g[i][None]
        g2 = norm2_g[i][None]
        rw2 = _router_rows(router_w[i])

        ctx_out = None
        if kind == 0:
            s_bias = jnp.repeat(g_spatial_b[j].T, LANES, axis=1)
            gm = functools.partial(
                _gmlp_layer, g1=g1, w_in=g_w_in[j].astype(BF16), v_g=g_v_norm_g[j][None],
                s_w=g_spatial_w[j].astype(BF16), s_bias=s_bias, w_out=g_w_out[j].astype(BF16),
                g2=g2, rw2=rw2)
            x1, h2, lg = gm(x_cur, x_cm, s, mod_lat)
            if update_ctx:
                ctx_out = gm(c_cur, c_cm, s_ctx, mod_ctx)
        elif kind == 1:
            assert not update_ctx
            cos, sin = _rope_tables(s)
            w_in = att_w_in[j].astype(BF16)
            qg, kg = att_q_norm_g[j][None], att_k_norm_g[j][None]
            q, k, v = _qkv_proj(x_cur, x_cm, s, mod_lat, g1, w_in, qg, kg, cos, sin, True)
            kc, vc = _qkv_proj(c_cur, c_cm, s_ctx, mod_ctx, g1, w_in, qg, kg,
                               jnp.ones((s_ctx, HEAD_DIM), F32), jnp.zeros((s_ctx, HEAD_DIM), F32), False)
            k_all = jnp.concatenate([kc, k], axis=1)
            v_all = jnp.concatenate([vc, v], axis=1)
            assert x_cm
            x1, h2, lg = _attn_layer(x_cur, s, q, k_all, v_all, mod_lat, att_w_out[j].astype(BF16), g2, rw2)
        else:
            assert not update_ctx and x_cm
            x1, h2, lg = _pool_layer(x_cur, s, mod_lat, g1, p_w_in[j].astype(BF16),
                                     p_w_group[j].astype(BF16), p_scale[j][None], g2, rw2)

        idx, gate = _route(lg, s)
        xin = _gather_lat(idx, h2, s)
        xin_c = None
        if ctx_out is not None:
            c1, hc2, lgc = ctx_out
            idx_c, gate_c = _route(lgc, s_ctx)
            xin_c = _gather_ctx(idx_c, hc2, s_ctx)
        ys = _moe_ffn(xin, xin_c, exp_w_gate, exp_w_up, exp_w_down, i, bsz)
        x_cur = _combine(idx, gate, x1, ys[0], mod_lat, s, False, last)
        x_cm = True
        if ctx_out is not None:
            c_cur = _combine(idx_c, gate_c, c1, ys[1], mod_ctx, s_ctx, True, False)
            c_cm = True
    return x_cur
```

```python
import functools
import math

import jax
import jax.numpy as jnp
from jax import lax
from jax.experimental import pallas as pl
from jax.experimental.pallas import tpu as pltpu

F32 = jnp.float32
BF16 = jnp.bfloat16

D_MODEL = 1024
LANES = 128
SUBLANES = 8
N_CHUNK = D_MODEL // LANES
ROW_PAD = 8
EPS = 1e-6
GRID_W = 64
CHUNK = 128
A_GROUPS = 8
HEAD_DIM = 128
N_HEADS = 8
N_KV_HEADS = 2
Q_PER_KV = N_HEADS // N_KV_HEADS
ROPE_THETA = 10000.0
POOL_WINDOWS = (2, 4, 8, 16)
POOL_GROUP_DIM = D_MODEL // len(POOL_WINDOWS)
POOL_HALO = 8
N_EXPERTS = 16
CAPACITY_FACTOR = 2
VMEM_LIMIT = 56 * 1024 * 1024


def _cparams(*sem):
    return pltpu.CompilerParams(dimension_semantics=sem, vmem_limit_bytes=VMEM_LIMIT)


def _dot(a, b):
    return jnp.dot(a, b, preferred_element_type=F32)


def _dot_nt(a, b):
    return lax.dot_general(a, b, (((1,), (1,)), ((), ())), preferred_element_type=F32)


def _chunks_to_tile(ref, lead, rows):
    return jnp.concatenate([ref[lead + (j, rows)] for j in range(N_CHUNK)], axis=-1)


def _tile_to_chunks(ref, lead, rows, val):
    for j in range(N_CHUNK):
        ref[lead + (j, rows)] = val[:, j * LANES:(j + 1) * LANES]


def _rms_mod(x, g, shift, scale):
    y = x * lax.rsqrt(jnp.mean(x * x, axis=-1, keepdims=True) + EPS)
    return (y * g) * (1.0 + scale) + shift


def _mod_kernel(c_ref, w_ref, b_ref, o_ref):
    a = jax.nn.silu(c_ref[...]).astype(BF16)
    o_ref[0] = _dot(a, w_ref[0].astype(BF16)) + b_ref[0]


def _modulation(cc, mod_w, mod_b):
    depth, d, n = mod_w.shape
    tn = 1536
    return pl.pallas_call(
        _mod_kernel,
        grid=(depth, n // tn),
        in_specs=[
            pl.BlockSpec((16, d), lambda l, j: (0, 0)),
            pl.BlockSpec((1, d, tn), lambda l, j: (l, 0, j)),
            pl.BlockSpec((1, 1, tn), lambda l, j: (l, 0, j)),
        ],
        out_specs=pl.BlockSpec((1, 16, tn), lambda l, j: (l, 0, j)),
        out_shape=jax.ShapeDtypeStruct((depth, 16, n), F32),
        compiler_params=_cparams("parallel", "parallel"),
        name="adaln_modulation",
    )(cc, mod_w, mod_b.reshape(depth, 1, n))


def _load_x(x_ref, x_cm):
    if x_cm:
        return _chunks_to_tile(x_ref, (0,), slice(None))
    return x_ref[0]


def _epilogue(x, y, mod_ref, g2_ref, rw_ref, x1_ref, h2_ref, lg_ref):
    x1 = x + mod_ref[0, 2:3, :] * y
    _tile_to_chunks(x1_ref, (0,), slice(None), x1)
    h2 = _rms_mod(x1, g2_ref[...], mod_ref[0, 3:4, :], mod_ref[0, 4:5, :])
    tm = h2.shape[0]
    for j in range(N_CHUNK):
        h2_ref[0, pl.ds(j, tm, stride=N_CHUNK), :] = h2[:, j * LANES:(j + 1) * LANES]
    hi = h2.astype(BF16)
    lo = (h2 - hi.astype(F32)).astype(BF16)
    rw = rw_ref[...]
    a = _dot_nt(rw, hi)
    b = _dot_nt(rw[:N_EXPERTS], lo)
    lg_ref[0] = a[:N_EXPERTS] + a[N_EXPERTS:] + b


def _x_spec(x_cm, tm, d):
    if x_cm:
        return pl.BlockSpec((1, N_CHUNK, tm, LANES), lambda b, i: (b, 0, i, 0))
    return pl.BlockSpec((1, tm, d), lambda b, i: (b, i, 0))


def _const_spec(shape):
    nd = len(shape)
    return pl.BlockSpec(shape, lambda b, i: (0,) * nd)


def _epilogue_specs(bsz, s, tm):
    cm_spec = pl.BlockSpec((1, N_CHUNK, tm, LANES), lambda b, i: (b, 0, i, 0))
    tt_spec = pl.BlockSpec((1, tm * N_CHUNK, LANES), lambda b, i: (b, i, 0))
    out_specs = [cm_spec, tt_spec, pl.BlockSpec((1, N_EXPERTS, tm), lambda b, i: (b, 0, i))]
    out_shape = [
        jax.ShapeDtypeStruct((bsz, N_CHUNK, s, LANES), F32),
        jax.ShapeDtypeStruct((bsz, s * N_CHUNK, LANES), F32),
        jax.ShapeDtypeStruct((bsz, N_EXPERTS, s), F32),
    ]
    return out_specs, out_shape


def _mod_spec():
    return pl.BlockSpec((1, 6, D_MODEL), lambda b, i: (b, 0, 0))


def _gmlp_kernel(x_ref, mod_ref, g1_ref, win_ref, vg_ref, sw_ref, sb_ref, wout_ref, g2_ref, rw_ref,
                 x1_ref, h2_ref, lg_ref, *, x_cm, tm):
    x = _load_x(x_ref, x_cm)
    h = _rms_mod(x, g1_ref[...], mod_ref[0, 0:1, :], mod_ref[0, 1:2, :]).astype(BF16)
    z = jax.nn.gelu(_dot(h, win_ref[...]))
    u = z[:, :D_MODEL]
    v = z[:, D_MODEL:]
    mu = jnp.mean(v, axis=-1, keepdims=True)
    vc = v - mu
    v = (vc * lax.rsqrt(jnp.mean(vc * vc, axis=-1, keepdims=True) + EPS)) * vg_ref[...]
    vb = v.astype(BF16)
    nc = tm // CHUNK
    mixed = []
    for g in range(A_GROUPS):
        cols = jnp.concatenate(
            [vb[c * CHUNK:(c + 1) * CHUNK, g * LANES:(g + 1) * LANES] for c in range(nc)], axis=1)
        mixed.append(_dot(sw_ref[g], cols))
    sv = jnp.concatenate(
        [jnp.concatenate([mixed[g][:, c * LANES:(c + 1) * LANES] for g in range(A_GROUPS)], axis=1)
         + sb_ref[...] for c in range(nc)], axis=0)
    y = _dot((u * sv).astype(BF16), wout_ref[...])
    _epilogue(x, y, mod_ref, g2_ref, rw_ref, x1_ref, h2_ref, lg_ref)


def _gmlp_layer(x, x_cm, s, mod, g1, w_in, v_g, s_w, s_bias, w_out, g2, rw2):
    bsz = x.shape[0]
    tm = min(512, s)
    out_specs, out_shape = _epilogue_specs(bsz, s, tm)
    return pl.pallas_call(
        functools.partial(_gmlp_kernel, x_cm=x_cm, tm=tm),
        grid=(bsz, s // tm),
        in_specs=[
            _x_spec(x_cm, tm, D_MODEL), _mod_spec(), _const_spec((1, D_MODEL)),
            _const_spec(w_in.shape), _const_spec((1, D_MODEL)), _const_spec(s_w.shape),
            _const_spec(s_bias.shape), _const_spec(w_out.shape), _const_spec((1, D_MODEL)),
            _const_spec(rw2.shape),
        ],
        out_specs=out_specs,
        out_shape=out_shape,
        compiler_params=_cparams("parallel", "arbitrary"),
        name="gmlp_mixer",
    )(x, mod, g1, w_in, v_g, s_w, s_bias, w_out, g2, rw2)


def _qkv_kernel(x_ref, mod_ref, g1_ref, w_ref, qg_ref, kg_ref, cos_ref, sin_ref, *out_refs,
                x_cm, want_q):
    x = _load_x(x_ref, x_cm)
    h = _rms_mod(x, g1_ref[...], mod_ref[0, 0:1, :], mod_ref[0, 1:2, :]).astype(BF16)
    qkv = _dot(h, w_ref[...])
    cos = cos_ref[...]
    sin = sin_ref[...]

    def norm_rope(t, g):
        t = (t * lax.rsqrt(jnp.mean(t * t, axis=-1, keepdims=True) + EPS)) * g
        return t * cos + pltpu.roll(t, HEAD_DIM // 2, 1) * sin

    qd = N_HEADS * HEAD_DIM
    kvd = N_KV_HEADS * HEAD_DIM
    if want_q:
        q_ref, k_ref, v_ref = out_refs
        for hd in range(N_HEADS):
            qh = norm_rope(qkv[:, hd * HEAD_DIM:(hd + 1) * HEAD_DIM], qg_ref[...])
            q_ref[0, :, hd * HEAD_DIM:(hd + 1) * HEAD_DIM] = (qh * (HEAD_DIM ** -0.5)).astype(BF16)
    else:
        k_ref, v_ref = out_refs
    for hd in range(N_KV_HEADS):
        kh = norm_rope(qkv[:, qd + hd * HEAD_DIM:qd + (hd + 1) * HEAD_DIM], kg_ref[...])
        k_ref[0, :, hd * HEAD_DIM:(hd + 1) * HEAD_DIM] = kh.astype(BF16)
    v_ref[0] = qkv[:, qd + kvd:].astype(BF16)


def _qkv_proj(x, x_cm, s, mod, g1, w_in, q_g, k_g, cos, sin, want_q):
    bsz = x.shape[0]
    tm = min(512, s)
    qd = N_HEADS * HEAD_DIM
    kvd = N_KV_HEADS * HEAD_DIM
    kv_spec = pl.BlockSpec((1, tm, kvd), lambda b, i: (b, i, 0))
    kv_shape = jax.ShapeDtypeStruct((bsz, s, kvd), BF16)
    out_specs = [kv_spec, kv_spec]
    out_shape = [kv_shape, kv_shape]
    if want_q:
        out_specs = [pl.BlockSpec((1, tm, qd), lambda b, i: (b, i, 0))] + out_specs
        out_shape = [jax.ShapeDtypeStruct((bsz, s, qd), BF16)] + out_shape
    return pl.pallas_call(
        functools.partial(_qkv_kernel, x_cm=x_cm, want_q=want_q),
        grid=(bsz, s // tm),
        in_specs=[
            _x_spec(x_cm, tm, D_MODEL), _mod_spec(), _const_spec((1, D_MODEL)),
            _const_spec(w_in.shape), _const_spec((1, HEAD_DIM)), _const_spec((1, HEAD_DIM)),
            pl.BlockSpec((tm, HEAD_DIM), lambda b, i: (i, 0)),
            pl.BlockSpec((tm, HEAD_DIM), lambda b, i: (i, 0)),
        ],
        out_specs=out_specs,
        out_shape=out_shape,
        compiler_params=_cparams("parallel", "arbitrary"),
        name="qkv_proj",
    )(x, mod, g1, w_in, q_g, k_g, cos, sin)


def _attn_kernel(x_ref, q_ref, k_ref, v_ref, mod_ref, wout_ref, g2_ref, rw_ref,
                 x1_ref, h2_ref, lg_ref, *, tq):
    x = _load_x(x_ref, True)
    heads = [None] * N_HEADS
    for kh in range(N_KV_HEADS):
        k = k_ref[0, :, kh * HEAD_DIM:(kh + 1) * HEAD_DIM]
        v = v_ref[0, :, kh * HEAD_DIM:(kh + 1) * HEAD_DIM]
        qg = jnp.concatenate(
            [q_ref[0, :, (kh * Q_PER_KV + g) * HEAD_DIM:(kh * Q_PER_KV + g + 1) * HEAD_DIM]
             for g in range(Q_PER_KV)], axis=0)
        sc = _dot_nt(qg, k)
        p = jnp.exp(sc - jnp.max(sc, axis=-1, keepdims=True))
        o = _dot(p.astype(BF16), v) / jnp.sum(p, axis=-1, keepdims=True)
        for g in range(Q_PER_KV):
            heads[kh * Q_PER_KV + g] = o[g * tq:(g + 1) * tq].astype(BF16)
    y = _dot(jnp.concatenate(heads, axis=1), wout_ref[...])
    _epilogue(x, y, mod_ref, g2_ref, rw_ref, x1_ref, h2_ref, lg_ref)


def _attn_layer(x, s, q, k_all, v_all, mod, w_out, g2, rw2):
    bsz = x.shape[0]
    tq = 128
    sk = k_all.shape[1]
    qd = N_HEADS * HEAD_DIM
    kvd = N_KV_HEADS * HEAD_DIM
    out_specs, out_shape = _epilogue_specs(bsz, s, tq)
    return pl.pallas_call(
        functools.partial(_attn_kernel, tq=tq),
        grid=(bsz, s // tq),
        in_specs=[
            _x_spec(True, tq, D_MODEL),
            pl.BlockSpec((1, tq, qd), lambda b, i: (b, i, 0)),
            pl.BlockSpec((1, sk, kvd), lambda b, i: (b, 0, 0)),
            pl.BlockSpec((1, sk, kvd), lambda b, i: (b, 0, 0)),
            _mod_spec(), _const_spec(w_out.shape), _const_spec((1, D_MODEL)), _const_spec(rw2.shape),
        ],
        out_specs=out_specs,
        out_shape=out_shape,
        compiler_params=_cparams("parallel", "arbitrary"),
        name="gqa_attention",
    )(x, q, k_all, v_all, mod, w_out, g2, rw2)


def _pool_kernel(xp_ref, x_ref, xn_ref, mod_ref, g1_ref, win_ref, wgrp_ref, ps_ref, g2_ref, rw_ref,
                 x1_ref, h2_ref, lg_ref, *, tm, s):
    i = pl.program_id(1)
    x = _load_x(x_ref, True)
    xe = jnp.concatenate([_load_x(xp_ref, True), x, _load_x(xn_ref, True)], axis=0)
    h = _rms_mod(xe, g1_ref[...], mod_ref[0, 0:1, :], mod_ref[0, 1:2, :]).astype(BF16)
    z = _dot(h, win_ref[...])
    te = tm + LANES
    zb = jnp.concatenate([z, jnp.zeros((te - tm - 2 * POOL_HALO, D_MODEL), F32)], axis=0).astype(BF16)
    pt = i * tm + lax.broadcasted_iota(jnp.int32, (tm, te), 0)
    col = lax.broadcasted_iota(jnp.int32, (tm, te), 1)
    ps = i * tm - POOL_HALO + col
    valid = (ps >= 0) & (ps < s) & (col < tm + 2 * POOL_HALO)
    tcol = i * tm + lax.broadcasted_iota(jnp.int32, (tm, 1), 0)
    outs = []
    for gi, w in enumerate(POOL_WINDOWS):
        half = w // 2
        band = jnp.where(valid & (ps >= pt - half) & (ps <= pt + half - 1), 1.0, 0.0).astype(BF16)
        cnt = (jnp.minimum(tcol + half - 1, s - 1) - jnp.maximum(tcol - half, 0) + 1).astype(F32)
        lo, hi_ = gi * POOL_GROUP_DIM, (gi + 1) * POOL_GROUP_DIM
        pooled = _dot(band, zb[:, lo:hi_]) / cnt - z[POOL_HALO:POOL_HALO + tm, lo:hi_]
        outs.append(_dot(pooled.astype(BF16), wgrp_ref[gi]))
    y = jnp.concatenate(outs, axis=1) * ps_ref[...]
    _epilogue(x, y, mod_ref, g2_ref, rw_ref, x1_ref, h2_ref, lg_ref)


def _pool_layer(x, s, mod, g1, w_in, w_grp, p_scale, g2, rw2):
    bsz = x.shape[0]
    tm = 256
    nb = tm // POOL_HALO
    last = s // POOL_HALO - 1
    out_specs, out_shape = _epilogue_specs(bsz, s, tm)
    halo = (1, N_CHUNK, POOL_HALO, LANES)
    return pl.pallas_call(
        functools.partial(_pool_kernel, tm=tm, s=s),
        grid=(bsz, s // tm),
        in_specs=[
            pl.BlockSpec(halo, lambda b, i: (b, 0, jnp.maximum(i * nb - 1, 0), 0)),
            _x_spec(True, tm, D_MODEL),
            pl.BlockSpec(halo, lambda b, i: (b, 0, jnp.minimum((i + 1) * nb, last), 0)),
            _mod_spec(), _const_spec((1, D_MODEL)), _const_spec(w_in.shape), _const_spec(w_grp.shape),
            _const_spec((1, D_MODEL)), _const_spec((1, D_MODEL)), _const_spec(rw2.shape),
        ],
        out_specs=out_specs,
        out_shape=out_shape,
        compiler_params=_cparams("parallel", "arbitrary"),
        name="pool_mixer",
    )(x, x, x, mod, g1, w_in, w_grp, p_scale, g2, rw2)


SORT_ROWS = 32


def _comes_first(a, ia, b, ib):
    return (a > b) | ((a == b) & (ia < ib))


def _bitonic_sort(keys, idxs, lane, n):
    nblk = len(keys)
    blk_bits = nblk.bit_length() - 1

    def forwards(c, k):
        if k < nblk:
            return (c & k) == 0
        if k >= n:
            return True
        return (lane & (k >> blk_bits)) == 0

    k = 2
    while k <= n:
        j = k // 2
        while j >= 1:
            if j < nblk:
                for c in range(nblk):
                    if c & j:
                        continue
                    p = c | j
                    fw = forwards(c, k)
                    keep = _comes_first(keys[c], idxs[c], keys[p], idxs[p])
                    if fw is False:
                        keep = jnp.logical_not(keep)
                    elif fw is not True:
                        keep = jnp.logical_xor(keep, jnp.logical_not(fw))
                    keys[c], keys[p] = jnp.where(keep, keys[c], keys[p]), jnp.where(keep, keys[p], keys[c])
                    idxs[c], idxs[p] = jnp.where(keep, idxs[c], idxs[p]), jnp.where(keep, idxs[p], idxs[c])
            else:
                jl = j >> blk_bits
                lower = (lane & jl) == 0
                for c in range(nblk):
                    fw = forwards(c, k)
                    later = jnp.logical_not(lower) if fw is True else jnp.logical_xor(lower, fw)
                    x, ix = keys[c], idxs[c]
                    px = jnp.where(lower, pltpu.roll(x, LANES - jl, 1), pltpu.roll(x, jl, 1))
                    pix = jnp.where(lower, pltpu.roll(ix, LANES - jl, 1), pltpu.roll(ix, jl, 1))
                    keep = jnp.logical_xor(_comes_first(x, ix, px, pix), later)
                    keys[c] = jnp.where(keep, x, px)
                    idxs[c] = jnp.where(keep, ix, pix)
            j //= 2
        k *= 2
    return keys, idxs


def _first_positions(blocks, lane, cap):
    nblk = len(blocks)
    per = cap // nblk
    out = []
    for v in range(-(-cap // LANES)):
        acc = None
        for c in range(v * (LANES // per), min(nblk, (v + 1) * (LANES // per))):
            off = (c * per) % LANES
            piece = blocks[c] if off == 0 else pltpu.roll(blocks[c], off, 1)
            acc = piece if acc is None else jnp.where(lane < off, acc, piece)
        out.append(acc)
    return out


def _route_kernel(lg_ref, idx_ref, gate_ref, aff_ref, *, s, cap):
    lg = lg_ref[...]
    ex = jnp.exp(lg - jnp.max(lg, axis=1, keepdims=True))
    aff = ex / jnp.sum(ex, axis=1, keepdims=True)
    n_rows = lg.shape[0] * N_EXPERTS
    aff_ref[...] = aff.reshape(n_rows, s)
    nblk = s // LANES
    grp = min(SORT_ROWS, n_rows) if nblk > 2 else n_rows
    lane = lax.broadcasted_iota(jnp.int32, (grp, LANES), 1)

    def sort_group(g, carry):
        rows = pl.ds(pl.multiple_of(g * grp, grp), grp)
        keys = [aff_ref[rows, c * LANES:(c + 1) * LANES] for c in range(nblk)]
        idxs = [lane + c * LANES for c in range(nblk)]
        keys, idxs = _bitonic_sort(keys, idxs, lane, s)
        top_keys = _first_positions(keys, lane, cap)
        top_idxs = _first_positions(idxs, lane, cap)
        for v in range(len(top_keys)):
            w = min(LANES, cap - v * LANES)
            gate_ref[rows, v * LANES:v * LANES + w] = top_keys[v][:, :w]
            idx_ref[rows, v * LANES:v * LANES + w] = top_idxs[v][:, :w]
        return carry

    lax.fori_loop(0, n_rows // grp, sort_group, 0)


def _route(logits_t, s):
    bsz = logits_t.shape[0]
    cap = CAPACITY_FACTOR * s // N_EXPERTS
    n_rows = bsz * N_EXPERTS
    return pl.pallas_call(
        functools.partial(_route_kernel, s=s, cap=cap),
        out_shape=[
            jax.ShapeDtypeStruct((n_rows, cap), jnp.int32),
            jax.ShapeDtypeStruct((n_rows, cap), F32),
        ],
        scratch_shapes=[pltpu.VMEM((n_rows, s), F32)],
        compiler_params=pltpu.CompilerParams(vmem_limit_bytes=VMEM_LIMIT),
        name="expert_choice_route",
    )(logits_t)


MOE_TILES = 2


def _moe_kernel(idx_cur, idx_nxt, h2_hbm, wg_hbm, wu_hbm, wd_hbm, o_ref,
                xin, stage_g, stage_u, stage_d, wg, wu, wd, sem_x, sem_w,
                *, layer, n_b, out_tiles, rows):
    e = pl.program_id(0)
    b = pl.program_id(1)
    step = e * n_b + b
    n_steps = N_EXPERTS * n_b
    n_rows = out_tiles * rows
    slot = step % 2

    def row_copy(idx_ref, r, dst_slot):
        src_row = pl.multiple_of(idx_ref[0, 0, r] * N_CHUNK, N_CHUNK)
        return pltpu.make_async_copy(h2_hbm.at[pl.ds(src_row, N_CHUNK), :],
                                     xin.at[dst_slot, pl.ds(r * N_CHUNK, N_CHUNK), :],
                                     sem_x.at[dst_slot])

    def wait_rows(s_):
        pltpu.make_async_copy(h2_hbm.at[pl.ds(0, n_rows * N_CHUNK), :], xin.at[s_], sem_x.at[s_]).wait()

    def weight_copies(expert, ws):
        return [pltpu.make_async_copy(w.at[layer, expert], st.at[ws], sem_w.at[ws])
                for w, st in ((wg_hbm, stage_g), (wu_hbm, stage_u), (wd_hbm, stage_d))]

    @pl.when(step == 0)
    def _():
        for cp in weight_copies(0, 0):
            cp.start()

        def issue(r, carry):
            row_copy(idx_cur, r, 0).start()
            return carry

        lax.fori_loop(0, n_rows, issue, 0)

    @pl.when(b == 0)
    def _():
        ws = e % 2
        for cp in weight_copies(e, ws):
            cp.wait()

        @pl.when(e + 1 < N_EXPERTS)
        def _():
            for cp in weight_copies(e + 1, 1 - ws):
                cp.start()

        rb = 128

        def cast(i, carry):
            rws = pl.ds(pl.multiple_of(i * rb, rb), rb)
            wg[rws, :] = stage_g[ws, rws, :].astype(BF16)
            wu[rws, :] = stage_u[ws, rws, :].astype(BF16)
            wd[rws, :] = stage_d[ws, rws, :].astype(BF16)
            return carry

        lax.fori_loop(0, D_MODEL // rb, cast, 0)

    wait_rows(slot)
    xs = xin.at[slot]
    x = jnp.concatenate([xs[pl.ds(j, n_rows, stride=N_CHUNK), :] for j in range(N_CHUNK)], axis=1).astype(BF16)
    hid = (jax.nn.silu(_dot(x, wg[...])) * _dot(x, wu[...])).astype(BF16)
    y = _dot(hid, wd[...])
    s2 = rows + ROW_PAD
    for n in range(out_tiles):
        for j in range(N_CHUNK):
            o_ref[n, pl.ds(j * s2, rows), :] = y[n * rows:(n + 1) * rows, j * LANES:(j + 1) * LANES]
            o_ref[n, pl.ds(j * s2 + rows, ROW_PAD), :] = jnp.zeros((ROW_PAD, LANES), F32)

    for r in range(n_rows):
        row_copy(idx_nxt, r, 1 - slot).start()

    @pl.when(step == n_steps - 1)
    def _():
        wait_rows(1 - slot)


def _moe_ffn(rows_glob, h2_tt, w_gate, w_up, w_down, layer, out_tiles, rows):
    n_b = rows_glob.shape[0] // N_EXPERTS
    n_rows = out_tiles * rows
    n_steps = N_EXPERTS * n_b
    s2 = rows + ROW_PAD
    wshape = w_gate.shape[2:]
    smem = functools.partial(pl.BlockSpec, memory_space=pltpu.SMEM)
    hbm = pl.BlockSpec(memory_space=pl.ANY)
    return pl.pallas_call(
        functools.partial(_moe_kernel, layer=layer, n_b=n_b, out_tiles=out_tiles, rows=rows),
        grid=(N_EXPERTS, n_b),
        in_specs=[
            smem((1, 1, n_rows), lambda e, b: (e * n_b + b, 0, 0)),
            smem((1, 1, n_rows), lambda e, b: (jnp.minimum(e * n_b + b + 1, n_steps - 1), 0, 0)),
            hbm, hbm, hbm, hbm,
        ],
        out_specs=pl.BlockSpec((out_tiles, N_CHUNK * s2, LANES), lambda e, b: (e * n_b + b, 0, 0)),
        out_shape=jax.ShapeDtypeStruct((n_steps * out_tiles, N_CHUNK * s2, LANES), F32),
        scratch_shapes=[
            pltpu.VMEM((2, n_rows * N_CHUNK, LANES), F32),
            pltpu.VMEM((2,) + wshape, F32), pltpu.VMEM((2,) + wshape, F32), pltpu.VMEM((2,) + wshape, F32),
            pltpu.VMEM(wshape, BF16), pltpu.VMEM(wshape, BF16), pltpu.VMEM(wshape, BF16),
            pltpu.SemaphoreType.DMA((2,)), pltpu.SemaphoreType.DMA((2,)),
        ],
        compiler_params=_cparams("arbitrary", "arbitrary"),
        name="expert_ffn",
    )(rows_glob, rows_glob, h2_tt, w_gate, w_up, w_down)


def _global_rows(idx, bsz, s, tiles):
    cap = idx.shape[-1]
    glob = idx.reshape(bsz, N_EXPERTS, cap) + (jnp.arange(bsz, dtype=jnp.int32) * s)[:, None, None]
    return jnp.swapaxes(glob, 0, 1).reshape(N_EXPERTS * (bsz // tiles), 1, tiles * cap)


COMBINE_UNROLL = 8


def _combine_kernel(idx_ref, gate_ref, x_ref, y_ref, mod_ref, o_ref, acc, *, s, sp, cap, s2, n_seg, to_rows):
    e = pl.program_id(1)
    y = y_ref.at[0]
    base = pl.program_id(0) * cap if n_seg > 1 else 0

    @pl.when(e == 0)
    def _():
        for j in range(N_CHUNK):
            acc[pl.ds(j * sp, s), :] = x_ref[0, j]
            acc[pl.ds(j * sp + s, ROW_PAD), :] = jnp.zeros((ROW_PAD, LANES), F32)

    g2 = mod_ref[0, 5]

    def body(i, carry):
        new = []
        for u in range(COMBINE_UNROLL):
            r = i * COMBINE_UNROLL + u
            t = idx_ref[0, 0, r]
            yrow = y[pl.ds(base + r, N_CHUNK, stride=s2), :]
            cur = acc[pl.ds(t, N_CHUNK, stride=sp), :]
            new.append((t, cur + (gate_ref[0, 0, r] * g2) * yrow))
        for t, val in new:
            acc[pl.ds(t, N_CHUNK, stride=sp), :] = val
        return carry

    lax.fori_loop(0, cap // COMBINE_UNROLL, body, 0)

    @pl.when(e == N_EXPERTS - 1)
    def _():
        for j in range(N_CHUNK):
            if to_rows:
                o_ref[0, :, j * LANES:(j + 1) * LANES] = acc[pl.ds(j * sp, s), :]
            else:
                o_ref[0, j] = acc[pl.ds(j * sp, s), :]


def _combine(idx, gate, x1cm, ycm, mod, s, ctx_tiles, to_rows):
    bsz = x1cm.shape[0]
    sp = s + ROW_PAD
    cap = idx.shape[-1]
    n_seg = bsz if ctx_tiles else 1
    s2 = n_seg * cap + ROW_PAD
    if ctx_tiles:
        y_spec = pl.BlockSpec((1,) + ycm.shape[1:], lambda b, e: (e, 0, 0))
    else:
        y_spec = pl.BlockSpec((1,) + ycm.shape[1:], lambda b, e: (e * bsz + b, 0, 0))
    smem = functools.partial(pl.BlockSpec, memory_space=pltpu.SMEM)
    cm_spec = pl.BlockSpec((1, N_CHUNK, s, LANES), lambda b, e: (b, 0, 0, 0))
    if to_rows:
        out_spec = pl.BlockSpec((1, s, D_MODEL), lambda b, e: (b, 0, 0))
        out_shape = jax.ShapeDtypeStruct((bsz, s, D_MODEL), F32)
    else:
        out_spec = cm_spec
        out_shape = jax.ShapeDtypeStruct((bsz, N_CHUNK, s, LANES), F32)
    return pl.pallas_call(
        functools.partial(_combine_kernel, s=s, sp=sp, cap=cap, s2=s2, n_seg=n_seg, to_rows=to_rows),
        grid=(bsz, N_EXPERTS),
        in_specs=[
            smem((1, 1, cap), lambda b, e: (b * N_EXPERTS + e, 0, 0)),
            smem((1, 1, cap), lambda b, e: (b * N_EXPERTS + e, 0, 0)),
            cm_spec,
            y_spec,
            pl.BlockSpec((1, 6, N_CHUNK, LANES), lambda b, e: (b, 0, 0, 0)),
        ],
        out_specs=out_spec,
        out_shape=out_shape,
        scratch_shapes=[pltpu.VMEM((N_CHUNK * sp, LANES), F32)],
        compiler_params=_cparams("parallel", "arbitrary"),
        name="combine_rows" if to_rows else "combine_cm",
    )(idx.reshape(bsz * N_EXPERTS, 1, cap), gate.reshape(bsz * N_EXPERTS, 1, cap),
      x1cm, ycm, mod.reshape(bsz, 6, N_CHUNK, LANES))


def _rope_tables(n):
    rows = n // GRID_W
    row = jnp.repeat(jnp.arange(rows, dtype=F32), GRID_W)
    col = jnp.tile(jnp.arange(GRID_W, dtype=F32), rows)
    n_freq = HEAD_DIM // 4
    inv = jnp.power(ROPE_THETA, -jnp.arange(n_freq, dtype=F32) / n_freq)
    ang = jnp.concatenate([row[:, None] * inv, col[:, None] * inv], axis=-1)
    cos, sin = jnp.cos(ang), jnp.sin(ang)
    return jnp.concatenate([cos, cos], axis=-1), jnp.concatenate([-sin, sin], axis=-1)


def _router_rows(router_w):
    wt = router_w.T
    hi = wt.astype(BF16)
    lo = (wt - hi.astype(F32)).astype(BF16)
    return jnp.concatenate([hi, lo], axis=0)


def kernel(x, c, ctx, c_ctx, mod_w, mod_b, norm1_g, norm2_g, router_w, exp_w_gate, exp_w_up, exp_w_down,
           g_w_in, g_v_norm_g, g_spatial_w, g_spatial_b, g_w_out, att_w_in, att_q_norm_g, att_k_norm_g,
           att_w_out, p_w_in, p_w_group, p_scale):
    bsz, s, d = x.shape
    s_ctx = ctx.shape[1]
    depth = mod_w.shape[0]
    assert d == D_MODEL and bsz <= 8 and bsz % MOE_TILES == 0 and s % 512 == 0 and s_ctx % 128 == 0

    cc = jnp.zeros((16, d), F32).at[:bsz].set(c).at[bsz].set(c_ctx)
    mods = _modulation(cc, mod_w, mod_b).reshape(depth, 16, 6, d)

    x_cur, x_cm = x, False
    c_cur, c_cm = ctx, False
    for i in range(depth):
        kind, j = i % 3, i // 3
        update_ctx = any(l % 3 == 1 for l in range(i + 1, depth))
        last = i == depth - 1
        mod_lat = mods[i, :bsz]
        mod_ctx = jnp.broadcast_to(mods[i, bsz][None], (bsz, 6, d))
        g1 = norm1_g[i][None]
        g2 = norm2_g[i][None]
        rw2 = _router_rows(router_w[i])

        ctx_out = None
        if kind == 0:
            s_bias = jnp.repeat(g_spatial_b[j].T, LANES, axis=1)
            gm = functools.partial(
                _gmlp_layer, g1=g1, w_in=g_w_in[j].astype(BF16), v_g=g_v_norm_g[j][None],
                s_w=g_spatial_w[j].astype(BF16), s_bias=s_bias, w_out=g_w_out[j].astype(BF16),
                g2=g2, rw2=rw2)
            x1, h2, lg = gm(x_cur, x_cm, s, mod_lat)
            if update_ctx:
                ctx_out = gm(c_cur, c_cm, s_ctx, mod_ctx)
        elif kind == 1:
            assert not update_ctx
            cos, sin = _rope_tables(s)
            w_in = att_w_in[j].astype(BF16)
            qg, kg = att_q_norm_g[j][None], att_k_norm_g[j][None]
            q, k, v = _qkv_proj(x_cur, x_cm, s, mod_lat, g1, w_in, qg, kg, cos, sin, True)
            kc, vc = _qkv_proj(c_cur, c_cm, s_ctx, mod_ctx, g1, w_in, qg, kg,
                               jnp.ones((s_ctx, HEAD_DIM), F32), jnp.zeros((s_ctx, HEAD_DIM), F32), False)
            k_all = jnp.concatenate([kc, k], axis=1)
            v_all = jnp.concatenate([vc, v], axis=1)
            assert x_cm
            x1, h2, lg = _attn_layer(x_cur, s, q, k_all, v_all, mod_lat, att_w_out[j].astype(BF16), g2, rw2)
        else:
            assert not update_ctx and x_cm
            x1, h2, lg = _pool_layer(x_cur, s, mod_lat, g1, p_w_in[j].astype(BF16),
                                     p_w_group[j].astype(BF16), p_scale[j][None], g2, rw2)

        experts = functools.partial(_moe_ffn, w_gate=exp_w_gate, w_up=exp_w_up, w_down=exp_w_down, layer=i)
        idx, gate = _route(lg, s)
        ys = experts(_global_rows(idx, bsz, s, MOE_TILES), h2.reshape(bsz * s * N_CHUNK, LANES),
                     out_tiles=MOE_TILES, rows=idx.shape[-1])
        x_cur = _combine(idx, gate, x1, ys, mod_lat, s, False, last)
        x_cm = True
        if ctx_out is not None:
            c1, hc2, lgc = ctx_out
            idx_c, gate_c = _route(lgc, s_ctx)
            ys_c = experts(_global_rows(idx_c, bsz, s_ctx, bsz), hc2.reshape(bsz * s_ctx * N_CHUNK, LANES),
                           out_tiles=1, rows=bsz * idx_c.shape[-1])
            c_cur = _combine(idx_c, gate_c, c1, ys_c, mod_ctx, s_ctx, True, False)
            c_cm = True
    return x_cur
```

```python
import functools
import math

import jax
import jax.numpy as jnp
from jax import lax
from jax.experimental import pallas as pl
from jax.experimental.pallas import tpu as pltpu

F32 = jnp.float32
BF16 = jnp.bfloat16

D_MODEL = 1024
LANES = 128
SUBLANES = 8
N_CHUNK = D_MODEL // LANES
ROW_PAD = 8
EPS = 1e-6
GRID_W = 64
CHUNK = 128
A_GROUPS = 8
HEAD_DIM = 128
N_HEADS = 8
N_KV_HEADS = 2
Q_PER_KV = N_HEADS // N_KV_HEADS
ROPE_THETA = 10000.0
LOG2_E = 1.4426950408889634
ATTN_KEY_BLOCK = 768
POOL_WINDOWS = (2, 4, 8, 16)
POOL_GROUP_DIM = D_MODEL // len(POOL_WINDOWS)
POOL_HALO = 8
N_EXPERTS = 16
CAPACITY_FACTOR = 2
VMEM_LIMIT = 56 * 1024 * 1024


def _cparams(*sem):
    return pltpu.CompilerParams(dimension_semantics=sem, vmem_limit_bytes=VMEM_LIMIT)


def _dot(a, b):
    return jnp.dot(a, b, preferred_element_type=F32)


def _dot_nt(a, b):
    return lax.dot_general(a, b, (((1,), (1,)), ((), ())), preferred_element_type=F32)


def _chunks_to_tile(ref, lead, rows):
    return jnp.concatenate([ref[lead + (j, rows)] for j in range(N_CHUNK)], axis=-1)


def _tile_to_chunks(ref, lead, rows, val):
    for j in range(N_CHUNK):
        ref[lead + (j, rows)] = val[:, j * LANES:(j + 1) * LANES]


def _rms_mod(x, g, shift, scale):
    y = x * lax.rsqrt(jnp.mean(x * x, axis=-1, keepdims=True) + EPS)
    return (y * g) * (1.0 + scale) + shift


def _mod_kernel(c_ref, w_ref, b_ref, o_ref):
    a = jax.nn.silu(c_ref[...]).astype(BF16)
    o_ref[0] = _dot(a, w_ref[0].astype(BF16)) + b_ref[0]


def _modulation(cc, mod_w, mod_b):
    depth, d, n = mod_w.shape
    tn = 1536
    return pl.pallas_call(
        _mod_kernel,
        grid=(depth, n // tn),
        in_specs=[
            pl.BlockSpec((16, d), lambda l, j: (0, 0)),
            pl.BlockSpec((1, d, tn), lambda l, j: (l, 0, j)),
            pl.BlockSpec((1, 1, tn), lambda l, j: (l, 0, j)),
        ],
        out_specs=pl.BlockSpec((1, 16, tn), lambda l, j: (l, 0, j)),
        out_shape=jax.ShapeDtypeStruct((depth, 16, n), F32),
        compiler_params=_cparams("parallel", "parallel"),
        name="adaln_modulation",
    )(cc, mod_w, mod_b.reshape(depth, 1, n))


def _load_x(x_ref, x_cm):
    if x_cm:
        return _chunks_to_tile(x_ref, (0,), slice(None))
    return x_ref[0]


def _epilogue(x, y, mod_ref, g2_ref, rw_ref, x1_ref, h2_ref, lg_ref):
    x1 = x + mod_ref[0, 2:3, :] * y
    _tile_to_chunks(x1_ref, (0,), slice(None), x1)
    h2 = _rms_mod(x1, g2_ref[...], mod_ref[0, 3:4, :], mod_ref[0, 4:5, :])
    tm = h2.shape[0]
    for j in range(N_CHUNK):
        h2_ref[0, pl.ds(j, tm, stride=N_CHUNK), :] = h2[:, j * LANES:(j + 1) * LANES]
    hi = h2.astype(BF16)
    lo = (h2 - hi.astype(F32)).astype(BF16)
    rw = rw_ref[...]
    a = _dot_nt(rw, hi)
    b = _dot_nt(rw[:N_EXPERTS], lo)
    lg_ref[0] = a[:N_EXPERTS] + a[N_EXPERTS:] + b


def _x_spec(x_cm, tm, d):
    if x_cm:
        return pl.BlockSpec((1, N_CHUNK, tm, LANES), lambda b, i: (b, 0, i, 0))
    return pl.BlockSpec((1, tm, d), lambda b, i: (b, i, 0))


def _const_spec(shape):
    nd = len(shape)
    return pl.BlockSpec(shape, lambda b, i: (0,) * nd)


def _epilogue_specs(bsz, s, tm):
    cm_spec = pl.BlockSpec((1, N_CHUNK, tm, LANES), lambda b, i: (b, 0, i, 0))
    tt_spec = pl.BlockSpec((1, tm * N_CHUNK, LANES), lambda b, i: (b, i, 0))
    out_specs = [cm_spec, tt_spec, pl.BlockSpec((1, N_EXPERTS, tm), lambda b, i: (b, 0, i))]
    out_shape = [
        jax.ShapeDtypeStruct((bsz, N_CHUNK, s, LANES), F32),
        jax.ShapeDtypeStruct((bsz, s * N_CHUNK, LANES), F32),
        jax.ShapeDtypeStruct((bsz, N_EXPERTS, s), F32),
    ]
    return out_specs, out_shape


def _mod_spec():
    return pl.BlockSpec((1, 6, D_MODEL), lambda b, i: (b, 0, 0))


def _gmlp_kernel(x_ref, mod_ref, g1_ref, win_ref, vg_ref, sw_ref, sb_ref, wout_ref, g2_ref, rw_ref,
                 x1_ref, h2_ref, lg_ref, *, x_cm, tm):
    x = _load_x(x_ref, x_cm)
    h = _rms_mod(x, g1_ref[...], mod_ref[0, 0:1, :], mod_ref[0, 1:2, :]).astype(BF16)
    z = jax.nn.gelu(_dot(h, win_ref[...]))
    u = z[:, :D_MODEL]
    v = z[:, D_MODEL:]
    mu = jnp.mean(v, axis=-1, keepdims=True)
    vc = v - mu
    v = (vc * lax.rsqrt(jnp.mean(vc * vc, axis=-1, keepdims=True) + EPS)) * vg_ref[...]
    vb = v.astype(BF16)
    nc = tm // CHUNK
    mixed = []
    for g in range(A_GROUPS):
        cols = jnp.concatenate(
            [vb[c * CHUNK:(c + 1) * CHUNK, g * LANES:(g + 1) * LANES] for c in range(nc)], axis=1)
        mixed.append(_dot(sw_ref[g], cols))
    sv = jnp.concatenate(
        [jnp.concatenate([mixed[g][:, c * LANES:(c + 1) * LANES] for g in range(A_GROUPS)], axis=1)
         + sb_ref[...] for c in range(nc)], axis=0)
    y = _dot((u * sv).astype(BF16), wout_ref[...])
    _epilogue(x, y, mod_ref, g2_ref, rw_ref, x1_ref, h2_ref, lg_ref)


def _gmlp_layer(x, x_cm, s, mod, g1, w_in, v_g, s_w, s_bias, w_out, g2, rw2):
    bsz = x.shape[0]
    tm = min(512, s)
    out_specs, out_shape = _epilogue_specs(bsz, s, tm)
    return pl.pallas_call(
        functools.partial(_gmlp_kernel, x_cm=x_cm, tm=tm),
        grid=(bsz, s // tm),
        in_specs=[
            _x_spec(x_cm, tm, D_MODEL), _mod_spec(), _const_spec((1, D_MODEL)),
            _const_spec(w_in.shape), _const_spec((1, D_MODEL)), _const_spec(s_w.shape),
            _const_spec(s_bias.shape), _const_spec(w_out.shape), _const_spec((1, D_MODEL)),
            _const_spec(rw2.shape),
        ],
        out_specs=out_specs,
        out_shape=out_shape,
        compiler_params=_cparams("parallel", "arbitrary"),
        name="gmlp_mixer",
    )(x, mod, g1, w_in, v_g, s_w, s_bias, w_out, g2, rw2)


def _qkv_kernel(x_ref, mod_ref, g1_ref, w_ref, qg_ref, kg_ref, cos_ref, sin_ref, *out_refs,
                x_cm, want_q):
    x = _load_x(x_ref, x_cm)
    h = _rms_mod(x, g1_ref[...], mod_ref[0, 0:1, :], mod_ref[0, 1:2, :]).astype(BF16)
    qkv = _dot(h, w_ref[...])
    cos = cos_ref[...]
    sin = sin_ref[...]

    def norm_rope(t, g):
        t = (t * lax.rsqrt(jnp.mean(t * t, axis=-1, keepdims=True) + EPS)) * g
        return t * cos + pltpu.roll(t, HEAD_DIM // 2, 1) * sin

    qd = N_HEADS * HEAD_DIM
    kvd = N_KV_HEADS * HEAD_DIM
    if want_q:
        q_ref, k_ref, v_ref = out_refs
        for hd in range(N_HEADS):
            qh = norm_rope(qkv[:, hd * HEAD_DIM:(hd + 1) * HEAD_DIM], qg_ref[...])
            q_ref[0, :, hd * HEAD_DIM:(hd + 1) * HEAD_DIM] = (qh * (HEAD_DIM ** -0.5 * LOG2_E)).astype(BF16)
    else:
        k_ref, v_ref = out_refs
    for hd in range(N_KV_HEADS):
        kh = norm_rope(qkv[:, qd + hd * HEAD_DIM:qd + (hd + 1) * HEAD_DIM], kg_ref[...])
        k_ref[0, :, hd * HEAD_DIM:(hd + 1) * HEAD_DIM] = kh.astype(BF16)
    v_ref[0] = qkv[:, qd + kvd:].astype(BF16)


def _qkv_proj(x, x_cm, s, mod, g1, w_in, q_g, k_g, cos, sin, want_q):
    bsz = x.shape[0]
    tm = min(512, s)
    qd = N_HEADS * HEAD_DIM
    kvd = N_KV_HEADS * HEAD_DIM
    kv_spec = pl.BlockSpec((1, tm, kvd), lambda b, i: (b, i, 0))
    kv_shape = jax.ShapeDtypeStruct((bsz, s, kvd), BF16)
    out_specs = [kv_spec, kv_spec]
    out_shape = [kv_shape, kv_shape]
    if want_q:
        out_specs = [pl.BlockSpec((1, tm, qd), lambda b, i: (b, i, 0))] + out_specs
        out_shape = [jax.ShapeDtypeStruct((bsz, s, qd), BF16)] + out_shape
    return pl.pallas_call(
        functools.partial(_qkv_kernel, x_cm=x_cm, want_q=want_q),
        grid=(bsz, s // tm),
        in_specs=[
            _x_spec(x_cm, tm, D_MODEL), _mod_spec(), _const_spec((1, D_MODEL)),
            _const_spec(w_in.shape), _const_spec((1, HEAD_DIM)), _const_spec((1, HEAD_DIM)),
            pl.BlockSpec((tm, HEAD_DIM), lambda b, i: (i, 0)),
            pl.BlockSpec((tm, HEAD_DIM), lambda b, i: (i, 0)),
        ],
        out_specs=out_specs,
        out_shape=out_shape,
        compiler_params=_cparams("parallel", "arbitrary"),
        name="qkv_proj",
    )(x, mod, g1, w_in, q_g, k_g, cos, sin)


def _attn_kernel(x_ref, q_ref, k_ref, v_ref, mod_ref, wout_ref, g2_ref, rw_ref,
                 x1_ref, h2_ref, lg_ref, *, tq):
    x = _load_x(x_ref, True)
    heads = [None] * N_HEADS
    sk = k_ref.shape[1]
    ones_col = jnp.where(lax.broadcasted_iota(jnp.int32, (sk, HEAD_DIM), 1) == 0, 1.0, 0.0).astype(BF16)
    for kh in range(N_KV_HEADS):
        k = k_ref[0, :, kh * HEAD_DIM:(kh + 1) * HEAD_DIM]
        v = jnp.concatenate([v_ref[0, :, kh * HEAD_DIM:(kh + 1) * HEAD_DIM], ones_col], axis=1)
        qg = jnp.concatenate(
            [q_ref[0, :, (kh * Q_PER_KV + g) * HEAD_DIM:(kh * Q_PER_KV + g + 1) * HEAD_DIM]
             for g in range(Q_PER_KV)], axis=0)
        m = jnp.full((Q_PER_KV * tq, 1), -jnp.inf, F32)
        o = jnp.zeros((Q_PER_KV * tq, 2 * HEAD_DIM), F32)
        for kb in range(0, sk, ATTN_KEY_BLOCK):
            sc = _dot_nt(qg, k[kb:kb + ATTN_KEY_BLOCK])
            m_new = jnp.maximum(m, jnp.max(sc, axis=-1, keepdims=True))
            p = jnp.exp2(sc - m_new).astype(BF16)
            o = o * jnp.exp2(m - m_new) + _dot(p, v[kb:kb + ATTN_KEY_BLOCK])
            m = m_new
        o = o[:, :HEAD_DIM] / o[:, HEAD_DIM:HEAD_DIM + 1]
        for g in range(Q_PER_KV):
            heads[kh * Q_PER_KV + g] = o[g * tq:(g + 1) * tq].astype(BF16)
    y = _dot(jnp.concatenate(heads, axis=1), wout_ref[...])
    _epilogue(x, y, mod_ref, g2_ref, rw_ref, x1_ref, h2_ref, lg_ref)


def _attn_layer(x, s, q, k_all, v_all, mod, w_out, g2, rw2):
    bsz = x.shape[0]
    tq = 512
    sk = k_all.shape[1]
    qd = N_HEADS * HEAD_DIM
    kvd = N_KV_HEADS * HEAD_DIM
    out_specs, out_shape = _epilogue_specs(bsz, s, tq)
    return pl.pallas_call(
        functools.partial(_attn_kernel, tq=tq),
        grid=(bsz, s // tq),
        in_specs=[
            _x_spec(True, tq, D_MODEL),
            pl.BlockSpec((1, tq, qd), lambda b, i: (b, i, 0)),
            pl.BlockSpec((1, sk, kvd), lambda b, i: (b, 0, 0)),
            pl.BlockSpec((1, sk, kvd), lambda b, i: (b, 0, 0)),
            _mod_spec(), _const_spec(w_out.shape), _const_spec((1, D_MODEL)), _const_spec(rw2.shape),
        ],
        out_specs=out_specs,
        out_shape=out_shape,
        compiler_params=_cparams("parallel", "arbitrary"),
        name="gqa_attention",
    )(x, q, k_all, v_all, mod, w_out, g2, rw2)


def _pool_kernel(xp_ref, x_ref, xn_ref, mod_ref, g1_ref, win_ref, wgrp_ref, ps_ref, g2_ref, rw_ref,
                 x1_ref, h2_ref, lg_ref, *, tm, s):
    i = pl.program_id(1)
    x = _load_x(x_ref, True)
    xe = jnp.concatenate([_load_x(xp_ref, True), x, _load_x(xn_ref, True)], axis=0)
    h = _rms_mod(xe, g1_ref[...], mod_ref[0, 0:1, :], mod_ref[0, 1:2, :]).astype(BF16)
    z = _dot(h, win_ref[...])
    te = tm + LANES
    zb = jnp.concatenate([z, jnp.zeros((te - tm - 2 * POOL_HALO, D_MODEL), F32)], axis=0).astype(BF16)
    pt = i * tm + lax.broadcasted_iota(jnp.int32, (tm, te), 0)
    col = lax.broadcasted_iota(jnp.int32, (tm, te), 1)
    ps = i * tm - POOL_HALO + col
    valid = (ps >= 0) & (ps < s) & (col < tm + 2 * POOL_HALO)
    tcol = i * tm + lax.broadcasted_iota(jnp.int32, (tm, 1), 0)
    outs = []
    for gi, w in enumerate(POOL_WINDOWS):
        half = w // 2
        band = jnp.where(valid & (ps >= pt - half) & (ps <= pt + half - 1), 1.0, 0.0).astype(BF16)
        cnt = (jnp.minimum(tcol + half - 1, s - 1) - jnp.maximum(tcol - half, 0) + 1).astype(F32)
        lo, hi_ = gi * POOL_GROUP_DIM, (gi + 1) * POOL_GROUP_DIM
        pooled = _dot(band, zb[:, lo:hi_]) / cnt - z[POOL_HALO:POOL_HALO + tm, lo:hi_]
        outs.append(_dot(pooled.astype(BF16), wgrp_ref[gi]))
    y = jnp.concatenate(outs, axis=1) * ps_ref[...]
    _epilogue(x, y, mod_ref, g2_ref, rw_ref, x1_ref, h2_ref, lg_ref)


def _pool_layer(x, s, mod, g1, w_in, w_grp, p_scale, g2, rw2):
    bsz = x.shape[0]
    tm = 256
    nb = tm // POOL_HALO
    last = s // POOL_HALO - 1
    out_specs, out_shape = _epilogue_specs(bsz, s, tm)
    halo = (1, N_CHUNK, POOL_HALO, LANES)
    return pl.pallas_call(
        functools.partial(_pool_kernel, tm=tm, s=s),
        grid=(bsz, s // tm),
        in_specs=[
            pl.BlockSpec(halo, lambda b, i: (b, 0, jnp.maximum(i * nb - 1, 0), 0)),
            _x_spec(True, tm, D_MODEL),
            pl.BlockSpec(halo, lambda b, i: (b, 0, jnp.minimum((i + 1) * nb, last), 0)),
            _mod_spec(), _const_spec((1, D_MODEL)), _const_spec(w_in.shape), _const_spec(w_grp.shape),
            _const_spec((1, D_MODEL)), _const_spec((1, D_MODEL)), _const_spec(rw2.shape),
        ],
        out_specs=out_specs,
        out_shape=out_shape,
        compiler_params=_cparams("parallel", "arbitrary"),
        name="pool_mixer",
    )(x, x, x, mod, g1, w_in, w_grp, p_scale, g2, rw2)


SORT_ROWS = 32


def _comes_first(a, ia, b, ib):
    return (a > b) | ((a == b) & (ia < ib))


def _bitonic_sort(keys, idxs, lane, n):
    nblk = len(keys)
    blk_bits = nblk.bit_length() - 1

    def forwards(c, k):
        if k < nblk:
            return (c & k) == 0
        if k >= n:
            return True
        return (lane & (k >> blk_bits)) == 0

    k = 2
    while k <= n:
        j = k // 2
        while j >= 1:
            if j < nblk:
                for c in range(nblk):
                    if c & j:
                        continue
                    p = c | j
                    fw = forwards(c, k)
                    keep = _comes_first(keys[c], idxs[c], keys[p], idxs[p])
                    if fw is False:
                        keep = jnp.logical_not(keep)
                    elif fw is not True:
                        keep = jnp.logical_xor(keep, jnp.logical_not(fw))
                    keys[c], keys[p] = jnp.where(keep, keys[c], keys[p]), jnp.where(keep, keys[p], keys[c])
                    idxs[c], idxs[p] = jnp.where(keep, idxs[c], idxs[p]), jnp.where(keep, idxs[p], idxs[c])
            else:
                jl = j >> blk_bits
                lower = (lane & jl) == 0
                for c in range(nblk):
                    fw = forwards(c, k)
                    later = jnp.logical_not(lower) if fw is True else jnp.logical_xor(lower, fw)
                    x, ix = keys[c], idxs[c]
                    px = jnp.where(lower, pltpu.roll(x, LANES - jl, 1), pltpu.roll(x, jl, 1))
                    pix = jnp.where(lower, pltpu.roll(ix, LANES - jl, 1), pltpu.roll(ix, jl, 1))
                    keep = jnp.logical_xor(_comes_first(x, ix, px, pix), later)
                    keys[c] = jnp.where(keep, x, px)
                    idxs[c] = jnp.where(keep, ix, pix)
            j //= 2
        k *= 2
    return keys, idxs


def _first_positions(blocks, lane, cap):
    nblk = len(blocks)
    per = cap // nblk
    out = []
    for v in range(-(-cap // LANES)):
        acc = None
        for c in range(v * (LANES // per), min(nblk, (v + 1) * (LANES // per))):
            off = (c * per) % LANES
            piece = blocks[c] if off == 0 else pltpu.roll(blocks[c], off, 1)
            acc = piece if acc is None else jnp.where(lane < off, acc, piece)
        out.append(acc)
    return out


def _route_kernel(lg_ref, idx_ref, gate_ref, aff_ref, *, s, cap):
    lg = lg_ref[...]
    ex = jnp.exp(lg - jnp.max(lg, axis=1, keepdims=True))
    aff = ex / jnp.sum(ex, axis=1, keepdims=True)
    n_rows = lg.shape[0] * N_EXPERTS
    aff_ref[...] = aff.reshape(n_rows, s)
    nblk = s // LANES
    grp = min(SORT_ROWS, n_rows) if nblk > 2 else n_rows
    lane = lax.broadcasted_iota(jnp.int32, (grp, LANES), 1)

    def sort_group(g, carry):
        rows = pl.ds(pl.multiple_of(g * grp, grp), grp)
        keys = [aff_ref[rows, c * LANES:(c + 1) * LANES] for c in range(nblk)]
        idxs = [lane + c * LANES for c in range(nblk)]
        keys, idxs = _bitonic_sort(keys, idxs, lane, s)
        top_keys = _first_positions(keys, lane, cap)
        top_idxs = _first_positions(idxs, lane, cap)
        for v in range(len(top_keys)):
            w = min(LANES, cap - v * LANES)
            gate_ref[rows, v * LANES:v * LANES + w] = top_keys[v][:, :w]
            idx_ref[rows, v * LANES:v * LANES + w] = top_idxs[v][:, :w]
        return carry

    lax.fori_loop(0, n_rows // grp, sort_group, 0)


def _route(logits_t, s):
    bsz = logits_t.shape[0]
    cap = CAPACITY_FACTOR * s // N_EXPERTS
    n_rows = bsz * N_EXPERTS
    return pl.pallas_call(
        functools.partial(_route_kernel, s=s, cap=cap),
        out_shape=[
            jax.ShapeDtypeStruct((n_rows, cap), jnp.int32),
            jax.ShapeDtypeStruct((n_rows, cap), F32),
        ],
        scratch_shapes=[pltpu.VMEM((n_rows, s), F32)],
        compiler_params=pltpu.CompilerParams(vmem_limit_bytes=VMEM_LIMIT),
        name="expert_choice_route",
    )(logits_t)


MOE_TILES = 2


def _moe_kernel(idx_cur, idx_nxt, h2_hbm, wg_hbm, wu_hbm, wd_hbm, o_ref,
                xin, stage_g, stage_u, stage_d, wg, wu, wd, sem_x, sem_w,
                *, layer, n_b, out_tiles, rows):
    e = pl.program_id(0)
    b = pl.program_id(1)
    step = e * n_b + b
    n_steps = N_EXPERTS * n_b
    n_rows = out_tiles * rows
    slot = step % 2

    def row_copy(idx_ref, r, dst_slot):
        src_row = pl.multiple_of(idx_ref[0, 0, r] * N_CHUNK, N_CHUNK)
        return pltpu.make_async_copy(h2_hbm.at[pl.ds(src_row, N_CHUNK), :],
                                     xin.at[dst_slot, pl.ds(r * N_CHUNK, N_CHUNK), :],
                                     sem_x.at[dst_slot])

    def wait_rows(s_):
        pltpu.make_async_copy(h2_hbm.at[pl.ds(0, n_rows * N_CHUNK), :], xin.at[s_], sem_x.at[s_]).wait()

    def weight_copies(expert, ws):
        return [pltpu.make_async_copy(w.at[layer, expert], st.at[ws], sem_w.at[ws])
                for w, st in ((wg_hbm, stage_g), (wu_hbm, stage_u), (wd_hbm, stage_d))]

    @pl.when(step == 0)
    def _():
        for cp in weight_copies(0, 0):
            cp.start()

        def issue(r, carry):
            row_copy(idx_cur, r, 0).start()
            return carry

        lax.fori_loop(0, n_rows, issue, 0)

    @pl.when(b == 0)
    def _():
        ws = e % 2
        for cp in weight_copies(e, ws):
            cp.wait()

        @pl.when(e + 1 < N_EXPERTS)
        def _():
            for cp in weight_copies(e + 1, 1 - ws):
                cp.start()

        rb = 128

        def cast(i, carry):
            rws = pl.ds(pl.multiple_of(i * rb, rb), rb)
            wg[rws, :] = stage_g[ws, rws, :].astype(BF16)
            wu[rws, :] = stage_u[ws, rws, :].astype(BF16)
            wd[rws, :] = stage_d[ws, rws, :].astype(BF16)
            return carry

        lax.fori_loop(0, D_MODEL // rb, cast, 0)

    wait_rows(slot)
    xs = xin.at[slot]
    x = jnp.concatenate([xs[pl.ds(j, n_rows, stride=N_CHUNK), :] for j in range(N_CHUNK)], axis=1).astype(BF16)
    hid = (jax.nn.silu(_dot(x, wg[...])) * _dot(x, wu[...])).astype(BF16)
    y = _dot(hid, wd[...])
    s2 = rows + ROW_PAD
    for n in range(out_tiles):
        for j in range(N_CHUNK):
            o_ref[0, 0, n, pl.ds(j * s2, rows), :] = y[n * rows:(n + 1) * rows, j * LANES:(j + 1) * LANES]
            o_ref[0, 0, n, pl.ds(j * s2 + rows, ROW_PAD), :] = jnp.zeros((ROW_PAD, LANES), F32)

    for r in range(n_rows):
        row_copy(idx_nxt, r, 1 - slot).start()

    @pl.when(step == n_steps - 1)
    def _():
        wait_rows(1 - slot)


def _moe_ffn(rows_glob, h2_tt, w_gate, w_up, w_down, layer, out_tiles, rows):
    n_b = rows_glob.shape[0] // N_EXPERTS
    n_rows = out_tiles * rows
    n_steps = N_EXPERTS * n_b
    s2 = rows + ROW_PAD
    wshape = w_gate.shape[2:]
    smem = functools.partial(pl.BlockSpec, memory_space=pltpu.SMEM)
    hbm = pl.BlockSpec(memory_space=pl.ANY)
    return pl.pallas_call(
        functools.partial(_moe_kernel, layer=layer, n_b=n_b, out_tiles=out_tiles, rows=rows),
        grid=(N_EXPERTS, n_b),
        in_specs=[
            smem((1, 1, n_rows), lambda e, b: (e * n_b + b, 0, 0)),
            smem((1, 1, n_rows), lambda e, b: (jnp.minimum(e * n_b + b + 1, n_steps - 1), 0, 0)),
            hbm, hbm, hbm, hbm,
        ],
        out_specs=pl.BlockSpec((1, 1, out_tiles, N_CHUNK * s2, LANES), lambda e, b: (b, e, 0, 0, 0)),
        out_shape=jax.ShapeDtypeStruct((n_b, N_EXPERTS, out_tiles, N_CHUNK * s2, LANES), F32),
        scratch_shapes=[
            pltpu.VMEM((2, n_rows * N_CHUNK, LANES), F32),
            pltpu.VMEM((2,) + wshape, F32), pltpu.VMEM((2,) + wshape, F32), pltpu.VMEM((2,) + wshape, F32),
            pltpu.VMEM(wshape, BF16), pltpu.VMEM(wshape, BF16), pltpu.VMEM(wshape, BF16),
            pltpu.SemaphoreType.DMA((2,)), pltpu.SemaphoreType.DMA((2,)),
        ],
        compiler_params=_cparams("arbitrary", "arbitrary"),
        name="expert_ffn",
    )(rows_glob, rows_glob, h2_tt, w_gate, w_up, w_down)


def _global_rows(idx, bsz, s, tiles):
    cap = idx.shape[-1]
    glob = idx.reshape(bsz, N_EXPERTS, cap) + (jnp.arange(bsz, dtype=jnp.int32) * s)[:, None, None]
    return jnp.swapaxes(glob, 0, 1).reshape(N_EXPERTS * (bsz // tiles), 1, tiles * cap)


COMBINE_UNROLL = 8
COMBINE_EXPERTS = 4


def _combine_kernel(idx_ref, gate_ref, x_ref, y_ref, mod_ref, o_ref, acc, *, s, sp, cap, s2, n_seg, to_rows):
    eg = pl.program_id(1)
    base = pl.program_id(0) * cap if n_seg > 1 else 0

    @pl.when(eg == 0)
    def _():
        for j in range(N_CHUNK):
            acc[pl.ds(j * sp, s), :] = x_ref[0, j]
            acc[pl.ds(j * sp + s, ROW_PAD), :] = jnp.zeros((ROW_PAD, LANES), F32)

    g2 = mod_ref[0, 5]

    for k in range(COMBINE_EXPERTS):
        y = y_ref.at[0, k, 0]

        def body(i, carry):
            new = []
            for u in range(COMBINE_UNROLL):
                r = i * COMBINE_UNROLL + u
                t = idx_ref[0, 0, k * cap + r]
                yrow = y[pl.ds(base + r, N_CHUNK, stride=s2), :]
                cur = acc[pl.ds(t, N_CHUNK, stride=sp), :]
                new.append((t, cur + (gate_ref[0, 0, k * cap + r] * g2) * yrow))
            for t, val in new:
                acc[pl.ds(t, N_CHUNK, stride=sp), :] = val
            return carry

        lax.fori_loop(0, cap // COMBINE_UNROLL, body, 0)

    @pl.when(eg == N_EXPERTS // COMBINE_EXPERTS - 1)
    def _():
        for j in range(N_CHUNK):
            if to_rows:
                o_ref[0, :, j * LANES:(j + 1) * LANES] = acc[pl.ds(j * sp, s), :]
            else:
                o_ref[0, j] = acc[pl.ds(j * sp, s), :]


def _combine(idx, gate, x1cm, ycm, mod, s, to_rows):
    bsz = x1cm.shape[0]
    sp = s + ROW_PAD
    cap = idx.shape[-1]
    n_b, _, tiles, rows8, _ = ycm.shape
    s2 = rows8 // N_CHUNK
    n_seg = bsz // (n_b * tiles)
    eg = COMBINE_EXPERTS
    y_spec = pl.BlockSpec((1, eg, 1, rows8, LANES), lambda b, g: (b // tiles // n_seg, g, b % tiles, 0, 0))
    smem = functools.partial(pl.BlockSpec, memory_space=pltpu.SMEM)
    cm_spec = pl.BlockSpec((1, N_CHUNK, s, LANES), lambda b, g: (b, 0, 0, 0))
    if to_rows:
        out_spec = pl.BlockSpec((1, s, D_MODEL), lambda b, g: (b, 0, 0))
        out_shape = jax.ShapeDtypeStruct((bsz, s, D_MODEL), F32)
    else:
        out_spec = cm_spec
        out_shape = jax.ShapeDtypeStruct((bsz, N_CHUNK, s, LANES), F32)
    n_grp = N_EXPERTS // eg
    return pl.pallas_call(
        functools.partial(_combine_kernel, s=s, sp=sp, cap=cap, s2=s2, n_seg=n_seg, to_rows=to_rows),
        grid=(bsz, n_grp),
        in_specs=[
            smem((1, 1, eg * cap), lambda b, g: (b * n_grp + g, 0, 0)),
            smem((1, 1, eg * cap), lambda b, g: (b * n_grp + g, 0, 0)),
            cm_spec,
            y_spec,
            pl.BlockSpec((1, 6, N_CHUNK, LANES), lambda b, g: (b, 0, 0, 0)),
        ],
        out_specs=out_spec,
        out_shape=out_shape,
        scratch_shapes=[pltpu.VMEM((N_CHUNK * sp, LANES), F32)],
        compiler_params=_cparams("parallel", "arbitrary"),
        name="combine_rows" if to_rows else "combine_cm",
    )(idx.reshape(bsz * n_grp, 1, eg * cap), gate.reshape(bsz * n_grp, 1, eg * cap),
      x1cm, ycm, mod.reshape(bsz, 6, N_CHUNK, LANES))


def _rope_tables(n):
    rows = n // GRID_W
    row = jnp.repeat(jnp.arange(rows, dtype=F32), GRID_W)
    col = jnp.tile(jnp.arange(GRID_W, dtype=F32), rows)
    n_freq = HEAD_DIM // 4
    inv = jnp.power(ROPE_THETA, -jnp.arange(n_freq, dtype=F32) / n_freq)
    ang = jnp.concatenate([row[:, None] * inv, col[:, None] * inv], axis=-1)
    cos, sin = jnp.cos(ang), jnp.sin(ang)
    return jnp.concatenate([cos, cos], axis=-1), jnp.concatenate([-sin, sin], axis=-1)


def _router_rows(router_w):
    wt = router_w.T
    hi = wt.astype(BF16)
    lo = (wt - hi.astype(F32)).astype(BF16)
    return jnp.concatenate([hi, lo], axis=0)


def kernel(x, c, ctx, c_ctx, mod_w, mod_b, norm1_g, norm2_g, router_w, exp_w_gate, exp_w_up, exp_w_down,
           g_w_in, g_v_norm_g, g_spatial_w, g_spatial_b, g_w_out, att_w_in, att_q_norm_g, att_k_norm_g,
           att_w_out, p_w_in, p_w_group, p_scale):
    bsz, s, d = x.shape
    s_ctx = ctx.shape[1]
    depth = mod_w.shape[0]
    assert d == D_MODEL and bsz <= 8 and bsz % MOE_TILES == 0 and s % 512 == 0 and s_ctx % 128 == 0

    cc = jnp.zeros((16, d), F32).at[:bsz].set(c).at[bsz].set(c_ctx)
    mods = _modulation(cc, mod_w, mod_b).reshape(depth, 16, 6, d)

    x_cur, x_cm = x, False
    c_cur, c_cm = ctx, False
    for i in range(depth):
        kind, j = i % 3, i // 3
        update_ctx = any(l % 3 == 1 for l in range(i + 1, depth))
        last = i == depth - 1
        mod_lat = mods[i, :bsz]
        mod_ctx = jnp.broadcast_to(mods[i, bsz][None], (bsz, 6, d))
        g1 = norm1_g[i][None]
        g2 = norm2_g[i][None]
        rw2 = _router_rows(router_w[i])

        ctx_out = None
        if kind == 0:
            s_bias = jnp.repeat(g_spatial_b[j].T, LANES, axis=1)
            gm = functools.partial(
                _gmlp_layer, g1=g1, w_in=g_w_in[j].astype(BF16), v_g=g_v_norm_g[j][None],
                s_w=g_spatial_w[j].astype(BF16), s_bias=s_bias, w_out=g_w_out[j].astype(BF16),
                g2=g2, rw2=rw2)
            x1, h2, lg = gm(x_cur, x_cm, s, mod_lat)
            if update_ctx:
                ctx_out = gm(c_cur, c_cm, s_ctx, mod_ctx)
        elif kind == 1:
            assert not update_ctx
            cos, sin = _rope_tables(s)
            w_in = att_w_in[j].astype(BF16)
            qg, kg = att_q_norm_g[j][None], att_k_norm_g[j][None]
            q, k, v = _qkv_proj(x_cur, x_cm, s, mod_lat, g1, w_in, qg, kg, cos, sin, True)
            kc, vc = _qkv_proj(c_cur, c_cm, s_ctx, mod_ctx, g1, w_in, qg, kg,
                               jnp.ones((s_ctx, HEAD_DIM), F32), jnp.zeros((s_ctx, HEAD_DIM), F32), False)
            k_all = jnp.concatenate([kc, k], axis=1)
            v_all = jnp.concatenate([vc, v], axis=1)
            assert x_cm
            x1, h2, lg = _attn_layer(x_cur, s, q, k_all, v_all, mod_lat, att_w_out[j].astype(BF16), g2, rw2)
        else:
            assert not update_ctx and x_cm
            x1, h2, lg = _pool_layer(x_cur, s, mod_lat, g1, p_w_in[j].astype(BF16),
                                     p_w_group[j].astype(BF16), p_scale[j][None], g2, rw2)

        experts = functools.partial(_moe_ffn, w_gate=exp_w_gate, w_up=exp_w_up, w_down=exp_w_down, layer=i)
        idx, gate = _route(lg, s)
        ys = experts(_global_rows(idx, bsz, s, MOE_TILES), h2.reshape(bsz * s * N_CHUNK, LANES),
                     out_tiles=MOE_TILES, rows=idx.shape[-1])
        x_cur = _combine(idx, gate, x1, ys, mod_lat, s, last)
        x_cm = True
        if ctx_out is not None:
            c1, hc2, lgc = ctx_out
            idx_c, gate_c = _route(lgc, s_ctx)
            ys_c = experts(_global_rows(idx_c, bsz, s_ctx, bsz), hc2.reshape(bsz * s_ctx * N_CHUNK, LANES),
                           out_tiles=1, rows=bsz * idx_c.shape[-1])
            c_cur = _combine(idx_c, gate_c, c1, ys_c, mod_ctx, s_ctx, False)
            c_cm = True
    return x_cur
```

```python
import functools
import math

import jax
import jax.numpy as jnp
from jax import lax
from jax.experimental import pallas as pl
from jax.experimental.pallas import tpu as pltpu

F32 = jnp.float32
BF16 = jnp.bfloat16

D_MODEL = 1024
LANES = 128
SUBLANES = 8
N_CHUNK = D_MODEL // LANES
ROW_PAD = 8
EPS = 1e-6
GRID_W = 64
CHUNK = 128
A_GROUPS = 8
HEAD_DIM = 128
N_HEADS = 8
N_KV_HEADS = 2
Q_PER_KV = N_HEADS // N_KV_HEADS
ROPE_THETA = 10000.0
LOG2_E = 1.4426950408889634
GMLP_SUB_TILE = 512
ATTN_KEY_BLOCK = 768
POOL_WINDOWS = (2, 4, 8, 16)
POOL_GROUP_DIM = D_MODEL // len(POOL_WINDOWS)
POOL_HALO = 8
N_EXPERTS = 16
CAPACITY_FACTOR = 2
VMEM_LIMIT = 56 * 1024 * 1024


def _cparams(*sem):
    return pltpu.CompilerParams(dimension_semantics=sem, vmem_limit_bytes=VMEM_LIMIT)


def _dot(a, b):
    return jnp.dot(a, b, preferred_element_type=F32)


def _dot_nt(a, b):
    return lax.dot_general(a, b, (((1,), (1,)), ((), ())), preferred_element_type=F32)


def _chunks_to_tile(ref, lead, rows):
    return jnp.concatenate([ref[lead + (j, rows)] for j in range(N_CHUNK)], axis=-1)


def _tile_to_chunks(ref, lead, rows, val):
    for j in range(N_CHUNK):
        ref[lead + (j, rows)] = val[:, j * LANES:(j + 1) * LANES]


def _rms_mod(x, g, shift, scale):
    y = x * lax.rsqrt(jnp.mean(x * x, axis=-1, keepdims=True) + EPS)
    return (y * g) * (1.0 + scale) + shift


def _mod_kernel(c_ref, w_ref, b_ref, o_ref):
    a = jax.nn.silu(c_ref[...]).astype(BF16)
    o_ref[0] = _dot(a, w_ref[0].astype(BF16)) + b_ref[0]


def _modulation(cc, mod_w, mod_b):
    depth, d, n = mod_w.shape
    tn = 1536
    return pl.pallas_call(
        _mod_kernel,
        grid=(depth, n // tn),
        in_specs=[
            pl.BlockSpec((16, d), lambda l, j: (0, 0)),
            pl.BlockSpec((1, d, tn), lambda l, j: (l, 0, j)),
            pl.BlockSpec((1, 1, tn), lambda l, j: (l, 0, j)),
        ],
        out_specs=pl.BlockSpec((1, 16, tn), lambda l, j: (l, 0, j)),
        out_shape=jax.ShapeDtypeStruct((depth, 16, n), F32),
        compiler_params=_cparams("parallel", "parallel"),
        name="adaln_modulation",
    )(cc, mod_w, mod_b.reshape(depth, 1, n))


def _load_x(x_ref, x_cm, r0=0, n=None):
    rows = slice(None) if n is None else slice(r0, r0 + n)
    if x_cm:
        return _chunks_to_tile(x_ref, (0,), rows)
    return x_ref[0, rows]


def _epilogue(x, y, mod_ref, g2_ref, rw_ref, x1_ref, h2_ref, lg_ref, r0=0):
    n = x.shape[0]
    x1 = x + mod_ref[0, 2:3, :] * y
    _tile_to_chunks(x1_ref, (0,), slice(r0, r0 + n), x1)
    h2 = _rms_mod(x1, g2_ref[...], mod_ref[0, 3:4, :], mod_ref[0, 4:5, :])
    for j in range(N_CHUNK):
        h2_ref[0, pl.ds(r0 * N_CHUNK + j, n, stride=N_CHUNK), :] = h2[:, j * LANES:(j + 1) * LANES]
    hi = h2.astype(BF16)
    lo = (h2 - hi.astype(F32)).astype(BF16)
    rw = rw_ref[...]
    a = _dot_nt(rw, hi)
    b = _dot_nt(rw[:N_EXPERTS], lo)
    lg_ref[0, :, r0:r0 + n] = a[:N_EXPERTS] + a[N_EXPERTS:] + b


def _x_spec(x_cm, tm, d):
    if x_cm:
        return pl.BlockSpec((1, N_CHUNK, tm, LANES), lambda b, i: (b, 0, i, 0))
    return pl.BlockSpec((1, tm, d), lambda b, i: (b, i, 0))


def _const_spec(shape):
    nd = len(shape)
    return pl.BlockSpec(shape, lambda b, i: (0,) * nd)


def _epilogue_specs(bsz, s, tm):
    cm_spec = pl.BlockSpec((1, N_CHUNK, tm, LANES), lambda b, i: (b, 0, i, 0))
    tt_spec = pl.BlockSpec((1, tm * N_CHUNK, LANES), lambda b, i: (b, i, 0))
    out_specs = [cm_spec, tt_spec, pl.BlockSpec((1, N_EXPERTS, tm), lambda b, i: (b, 0, i))]
    out_shape = [
        jax.ShapeDtypeStruct((bsz, N_CHUNK, s, LANES), F32),
        jax.ShapeDtypeStruct((bsz, s * N_CHUNK, LANES), F32),
        jax.ShapeDtypeStruct((bsz, N_EXPERTS, s), F32),
    ]
    return out_specs, out_shape


def _mod_spec():
    return pl.BlockSpec((1, 6, D_MODEL), lambda b, i: (b, 0, 0))


def _gmlp_kernel(x_ref, mod_ref, g1_ref, win_ref, vg_ref, sw_ref, sb_ref, wout_ref, g2_ref, rw_ref,
                 x1_ref, h2_ref, lg_ref, *, x_cm, tm):
    sub = min(tm, GMLP_SUB_TILE)
    nc = sub // CHUNK
    for r0 in range(0, tm, sub):
        x = _load_x(x_ref, x_cm, r0, sub)
        h = _rms_mod(x, g1_ref[...], mod_ref[0, 0:1, :], mod_ref[0, 1:2, :]).astype(BF16)
        z = jax.nn.gelu(_dot(h, win_ref[...]))
        u = z[:, :D_MODEL]
        v = z[:, D_MODEL:]
        mu = jnp.mean(v, axis=-1, keepdims=True)
        vc = v - mu
        v = (vc * lax.rsqrt(jnp.mean(vc * vc, axis=-1, keepdims=True) + EPS)) * vg_ref[...]
        vb = v.astype(BF16)
        mixed = []
        for g in range(A_GROUPS):
            cols = jnp.concatenate(
                [vb[c * CHUNK:(c + 1) * CHUNK, g * LANES:(g + 1) * LANES] for c in range(nc)], axis=1)
            mixed.append(_dot(sw_ref[g], cols))
        sv = jnp.concatenate(
            [jnp.concatenate([mixed[g][:, c * LANES:(c + 1) * LANES] for g in range(A_GROUPS)], axis=1)
             + sb_ref[...] for c in range(nc)], axis=0)
        y = _dot((u * sv).astype(BF16), wout_ref[...])
        _epilogue(x, y, mod_ref, g2_ref, rw_ref, x1_ref, h2_ref, lg_ref, r0)


def _gmlp_layer(x, x_cm, s, mod, g1, w_in, v_g, s_w, s_bias, w_out, g2, rw2):
    bsz = x.shape[0]
    tm = min(1024, s)
    out_specs, out_shape = _epilogue_specs(bsz, s, tm)
    return pl.pallas_call(
        functools.partial(_gmlp_kernel, x_cm=x_cm, tm=tm),
        grid=(bsz, s // tm),
        in_specs=[
            _x_spec(x_cm, tm, D_MODEL), _mod_spec(), _const_spec((1, D_MODEL)),
            _const_spec(w_in.shape), _const_spec((1, D_MODEL)), _const_spec(s_w.shape),
            _const_spec(s_bias.shape), _const_spec(w_out.shape), _const_spec((1, D_MODEL)),
            _const_spec(rw2.shape),
        ],
        out_specs=out_specs,
        out_shape=out_shape,
        compiler_params=_cparams("parallel", "arbitrary"),
        name="gmlp_mixer",
    )(x, mod, g1, w_in, v_g, s_w, s_bias, w_out, g2, rw2)


def _qkv_kernel(x_ref, mod_ref, g1_ref, w_ref, qg_ref, kg_ref, cos_ref, sin_ref, *out_refs,
                x_cm, want_q):
    x = _load_x(x_ref, x_cm)
    h = _rms_mod(x, g1_ref[...], mod_ref[0, 0:1, :], mod_ref[0, 1:2, :]).astype(BF16)
    qkv = _dot(h, w_ref[...])
    cos = cos_ref[...]
    sin = sin_ref[...]

    def norm_rope(t, g):
        t = (t * lax.rsqrt(jnp.mean(t * t, axis=-1, keepdims=True) + EPS)) * g
        return t * cos + pltpu.roll(t, HEAD_DIM // 2, 1) * sin

    qd = N_HEADS * HEAD_DIM
    kvd = N_KV_HEADS * HEAD_DIM
    if want_q:
        q_ref, k_ref, v_ref = out_refs
        for hd in range(N_HEADS):
            qh = norm_rope(qkv[:, hd * HEAD_DIM:(hd + 1) * HEAD_DIM], qg_ref[...])
            q_ref[0, :, hd * HEAD_DIM:(hd + 1) * HEAD_DIM] = (qh * (HEAD_DIM ** -0.5 * LOG2_E)).astype(BF16)
    else:
        k_ref, v_ref = out_refs
    for hd in range(N_KV_HEADS):
        kh = norm_rope(qkv[:, qd + hd * HEAD_DIM:qd + (hd + 1) * HEAD_DIM], kg_ref[...])
        k_ref[0, :, hd * HEAD_DIM:(hd + 1) * HEAD_DIM] = kh.astype(BF16)
    v_ref[0] = qkv[:, qd + kvd:].astype(BF16)


def _qkv_proj(x, x_cm, s, mod, g1, w_in, q_g, k_g, cos, sin, want_q):
    bsz = x.shape[0]
    tm = min(512, s)
    qd = N_HEADS * HEAD_DIM
    kvd = N_KV_HEADS * HEAD_DIM
    kv_spec = pl.BlockSpec((1, tm, kvd), lambda b, i: (b, i, 0))
    kv_shape = jax.ShapeDtypeStruct((bsz, s, kvd), BF16)
    out_specs = [kv_spec, kv_spec]
    out_shape = [kv_shape, kv_shape]
    if want_q:
        out_specs = [pl.BlockSpec((1, tm, qd), lambda b, i: (b, i, 0))] + out_specs
        out_shape = [jax.ShapeDtypeStruct((bsz, s, qd), BF16)] + out_shape
    return pl.pallas_call(
        functools.partial(_qkv_kernel, x_cm=x_cm, want_q=want_q),
        grid=(bsz, s // tm),
        in_specs=[
            _x_spec(x_cm, tm, D_MODEL), _mod_spec(), _const_spec((1, D_MODEL)),
            _const_spec(w_in.shape), _const_spec((1, HEAD_DIM)), _const_spec((1, HEAD_DIM)),
            pl.BlockSpec((tm, HEAD_DIM), lambda b, i: (i, 0)),
            pl.BlockSpec((tm, HEAD_DIM), lambda b, i: (i, 0)),
        ],
        out_specs=out_specs,
        out_shape=out_shape,
        compiler_params=_cparams("parallel", "arbitrary"),
        name="qkv_proj",
    )(x, mod, g1, w_in, q_g, k_g, cos, sin)


def _attn_kernel(x_ref, q_ref, k_ref, v_ref, mod_ref, wout_ref, g2_ref, rw_ref,
                 x1_ref, h2_ref, lg_ref, *, tq):
    x = _load_x(x_ref, True)
    heads = [None] * N_HEADS
    sk = k_ref.shape[1]
    ones_col = jnp.where(lax.broadcasted_iota(jnp.int32, (sk, HEAD_DIM), 1) == 0, 1.0, 0.0).astype(BF16)
    for kh in range(N_KV_HEADS):
        k = k_ref[0, :, kh * HEAD_DIM:(kh + 1) * HEAD_DIM]
        v = jnp.concatenate([v_ref[0, :, kh * HEAD_DIM:(kh + 1) * HEAD_DIM], ones_col], axis=1)
        qg = jnp.concatenate(
            [q_ref[0, :, (kh * Q_PER_KV + g) * HEAD_DIM:(kh * Q_PER_KV + g + 1) * HEAD_DIM]
             for g in range(Q_PER_KV)], axis=0)
        m = jnp.full((Q_PER_KV * tq, 1), -jnp.inf, F32)
        o = jnp.zeros((Q_PER_KV * tq, 2 * HEAD_DIM), F32)
        for kb in range(0, sk, ATTN_KEY_BLOCK):
            sc = _dot_nt(qg, k[kb:kb + ATTN_KEY_BLOCK])
            m_new = jnp.maximum(m, jnp.max(sc, axis=-1, keepdims=True))
            p = jnp.exp2(sc - m_new).astype(BF16)
            o = o * jnp.exp2(m - m_new) + _dot(p, v[kb:kb + ATTN_KEY_BLOCK])
            m = m_new
        o = o[:, :HEAD_DIM] / o[:, HEAD_DIM:HEAD_DIM + 1]
        for g in range(Q_PER_KV):
            heads[kh * Q_PER_KV + g] = o[g * tq:(g + 1) * tq].astype(BF16)
    y = _dot(jnp.concatenate(heads, axis=1), wout_ref[...])
    _epilogue(x, y, mod_ref, g2_ref, rw_ref, x1_ref, h2_ref, lg_ref)


def _attn_layer(x, s, q, k_all, v_all, mod, w_out, g2, rw2):
    bsz = x.shape[0]
    tq = 512
    sk = k_all.shape[1]
    qd = N_HEADS * HEAD_DIM
    kvd = N_KV_HEADS * HEAD_DIM
    out_specs, out_shape = _epilogue_specs(bsz, s, tq)
    return pl.pallas_call(
        functools.partial(_attn_kernel, tq=tq),
        grid=(bsz, s // tq),
        in_specs=[
            _x_spec(True, tq, D_MODEL),
            pl.BlockSpec((1, tq, qd), lambda b, i: (b, i, 0)),
            pl.BlockSpec((1, sk, kvd), lambda b, i: (b, 0, 0)),
            pl.BlockSpec((1, sk, kvd), lambda b, i: (b, 0, 0)),
            _mod_spec(), _const_spec(w_out.shape), _const_spec((1, D_MODEL)), _const_spec(rw2.shape),
        ],
        out_specs=out_specs,
        out_shape=out_shape,
        compiler_params=_cparams("parallel", "arbitrary"),
        name="gqa_attention",
    )(x, q, k_all, v_all, mod, w_out, g2, rw2)


def _pool_kernel(xp_ref, x_ref, xn_ref, mod_ref, g1_ref, win_ref, wgrp_ref, ps_ref, g2_ref, rw_ref,
                 x1_ref, h2_ref, lg_ref, *, tm, s):
    i = pl.program_id(1)
    x = _load_x(x_ref, True)
    xe = jnp.concatenate([_load_x(xp_ref, True), x, _load_x(xn_ref, True)], axis=0)
    h = _rms_mod(xe, g1_ref[...], mod_ref[0, 0:1, :], mod_ref[0, 1:2, :]).astype(BF16)
    z = _dot(h, win_ref[...])
    te = tm + LANES
    zb = jnp.concatenate([z, jnp.zeros((te - tm - 2 * POOL_HALO, D_MODEL), F32)], axis=0).astype(BF16)
    pt = i * tm + lax.broadcasted_iota(jnp.int32, (tm, te), 0)
    col = lax.broadcasted_iota(jnp.int32, (tm, te), 1)
    ps = i * tm - POOL_HALO + col
    valid = (ps >= 0) & (ps < s) & (col < tm + 2 * POOL_HALO)
    tcol = i * tm + lax.broadcasted_iota(jnp.int32, (tm, 1), 0)
    outs = []
    for gi, w in enumerate(POOL_WINDOWS):
        half = w // 2
        band = jnp.where(valid & (ps >= pt - half) & (ps <= pt + half - 1), 1.0, 0.0).astype(BF16)
        cnt = (jnp.minimum(tcol + half - 1, s - 1) - jnp.maximum(tcol - half, 0) + 1).astype(F32)
        lo, hi_ = gi * POOL_GROUP_DIM, (gi + 1) * POOL_GROUP_DIM
        pooled = _dot(band, zb[:, lo:hi_]) / cnt - z[POOL_HALO:POOL_HALO + tm, lo:hi_]
        outs.append(_dot(pooled.astype(BF16), wgrp_ref[gi]))
    y = jnp.concatenate(outs, axis=1) * ps_ref[...]
    _epilogue(x, y, mod_ref, g2_ref, rw_ref, x1_ref, h2_ref, lg_ref)


def _pool_layer(x, s, mod, g1, w_in, w_grp, p_scale, g2, rw2):
    bsz = x.shape[0]
    tm = 256
    nb = tm // POOL_HALO
    last = s // POOL_HALO - 1
    out_specs, out_shape = _epilogue_specs(bsz, s, tm)
    halo = (1, N_CHUNK, POOL_HALO, LANES)
    return pl.pallas_call(
        functools.partial(_pool_kernel, tm=tm, s=s),
        grid=(bsz, s // tm),
        in_specs=[
            pl.BlockSpec(halo, lambda b, i: (b, 0, jnp.maximum(i * nb - 1, 0), 0)),
            _x_spec(True, tm, D_MODEL),
            pl.BlockSpec(halo, lambda b, i: (b, 0, jnp.minimum((i + 1) * nb, last), 0)),
            _mod_spec(), _const_spec((1, D_MODEL)), _const_spec(w_in.shape), _const_spec(w_grp.shape),
            _const_spec((1, D_MODEL)), _const_spec((1, D_MODEL)), _const_spec(rw2.shape),
        ],
        out_specs=out_specs,
        out_shape=out_shape,
        compiler_params=_cparams("parallel", "arbitrary"),
        name="pool_mixer",
    )(x, x, x, mod, g1, w_in, w_grp, p_scale, g2, rw2)


SORT_ROWS = 32


def _comes_first(a, ia, b, ib):
    return (a > b) | ((a == b) & (ia < ib))


def _bitonic_sort(keys, idxs, lane, n):
    nblk = len(keys)
    blk_bits = nblk.bit_length() - 1

    def forwards(c, k):
        if k < nblk:
            return (c & k) == 0
        if k >= n:
            return True
        return (lane & (k >> blk_bits)) == 0

    k = 2
    while k <= n:
        j = k // 2
        while j >= 1:
            if j < nblk:
                for c in range(nblk):
                    if c & j:
                        continue
                    p = c | j
                    fw = forwards(c, k)
                    keep = _comes_first(keys[c], idxs[c], keys[p], idxs[p])
                    if fw is False:
                        keep = jnp.logical_not(keep)
                    elif fw is not True:
                        keep = jnp.logical_xor(keep, jnp.logical_not(fw))
                    keys[c], keys[p] = jnp.where(keep, keys[c], keys[p]), jnp.where(keep, keys[p], keys[c])
                    idxs[c], idxs[p] = jnp.where(keep, idxs[c], idxs[p]), jnp.where(keep, idxs[p], idxs[c])
            else:
                jl = j >> blk_bits
                lower = (lane & jl) == 0
                for c in range(nblk):
                    fw = forwards(c, k)
                    later = jnp.logical_not(lower) if fw is True else jnp.logical_xor(lower, fw)
                    x, ix = keys[c], idxs[c]
                    px = jnp.where(lower, pltpu.roll(x, LANES - jl, 1), pltpu.roll(x, jl, 1))
                    pix = jnp.where(lower, pltpu.roll(ix, LANES - jl, 1), pltpu.roll(ix, jl, 1))
                    keep = jnp.logical_xor(_comes_first(x, ix, px, pix), later)
                    keys[c] = jnp.where(keep, x, px)
                    idxs[c] = jnp.where(keep, ix, pix)
            j //= 2
        k *= 2
    return keys, idxs


def _first_positions(blocks, lane, cap):
    nblk = len(blocks)
    per = cap // nblk
    out = []
    for v in range(-(-cap // LANES)):
        acc = None
        for c in range(v * (LANES // per), min(nblk, (v + 1) * (LANES // per))):
            off = (c * per) % LANES
            piece = blocks[c] if off == 0 else pltpu.roll(blocks[c], off, 1)
            acc = piece if acc is None else jnp.where(lane < off, acc, piece)
        out.append(acc)
    return out


def _route_kernel(lg_ref, idx_ref, gate_ref, aff_ref, *, s, cap):
    lg = lg_ref[...]
    ex = jnp.exp(lg - jnp.max(lg, axis=1, keepdims=True))
    aff = ex / jnp.sum(ex, axis=1, keepdims=True)
    n_rows = lg.shape[0] * N_EXPERTS
    aff_ref[...] = aff.reshape(n_rows, s)
    nblk = s // LANES
    grp = min(SORT_ROWS, n_rows) if nblk > 2 else n_rows
    lane = lax.broadcasted_iota(jnp.int32, (grp, LANES), 1)

    def sort_group(g, carry):
        rows = pl.ds(pl.multiple_of(g * grp, grp), grp)
        keys = [aff_ref[rows, c * LANES:(c + 1) * LANES] for c in range(nblk)]
        idxs = [lane + c * LANES for c in range(nblk)]
        keys, idxs = _bitonic_sort(keys, idxs, lane, s)
        top_keys = _first_positions(keys, lane, cap)
        top_idxs = _first_positions(idxs, lane, cap)
        for v in range(len(top_keys)):
            w = min(LANES, cap - v * LANES)
            gate_ref[rows, v * LANES:v * LANES + w] = top_keys[v][:, :w]
            idx_ref[rows, v * LANES:v * LANES + w] = top_idxs[v][:, :w]
        return carry

    lax.fori_loop(0, n_rows // grp, sort_group, 0)


def _route(logits_t, s):
    bsz = logits_t.shape[0]
    cap = CAPACITY_FACTOR * s // N_EXPERTS
    n_rows = bsz * N_EXPERTS
    return pl.pallas_call(
        functools.partial(_route_kernel, s=s, cap=cap),
        out_shape=[
            jax.ShapeDtypeStruct((n_rows, cap), jnp.int32),
            jax.ShapeDtypeStruct((n_rows, cap), F32),
        ],
        scratch_shapes=[pltpu.VMEM((n_rows, s), F32)],
        compiler_params=pltpu.CompilerParams(vmem_limit_bytes=VMEM_LIMIT),
        name="expert_choice_route",
    )(logits_t)


MOE_TILES = 2
ROW_SLOTS = 3


def _moe_kernel(idx_cur, idx_nxt, idx_nxt2, h2_hbm, wg_hbm, wu_hbm, wd_hbm, o_ref,
                xin, stage_g, stage_u, stage_d, wg, wu, wd, sem_x, sem_w,
                *, layer, n_b, out_tiles, rows):
    e = pl.program_id(0)
    b = pl.program_id(1)
    step = e * n_b + b
    n_steps = N_EXPERTS * n_b
    n_rows = out_tiles * rows
    slot = step % ROW_SLOTS
    ahead = (step + ROW_SLOTS - 1) % ROW_SLOTS

    def row_copy(idx_ref, r, dst_slot):
        src_row = pl.multiple_of(idx_ref[0, 0, r] * N_CHUNK, N_CHUNK)
        return pltpu.make_async_copy(h2_hbm.at[pl.ds(src_row, N_CHUNK), :],
                                     xin.at[dst_slot, pl.ds(r * N_CHUNK, N_CHUNK), :],
                                     sem_x.at[dst_slot])

    def wait_rows(s_):
        pltpu.make_async_copy(h2_hbm.at[pl.ds(0, n_rows * N_CHUNK), :], xin.at[s_], sem_x.at[s_]).wait()

    def weight_copies(expert, ws):
        return [pltpu.make_async_copy(w.at[layer, expert], st.at[ws], sem_w.at[ws])
                for w, st in ((wg_hbm, stage_g), (wu_hbm, stage_u), (wd_hbm, stage_d))]

    @pl.when(step == 0)
    def _():
        for cp in weight_copies(0, 0):
            cp.start()

        def issue(r, carry):
            row_copy(idx_cur, r, 0).start()
            row_copy(idx_nxt, r, 1).start()
            return carry

        lax.fori_loop(0, n_rows, issue, 0)

    @pl.when(b == 0)
    def _():
        ws = e % 2
        for cp in weight_copies(e, ws):
            cp.wait()

        @pl.when(e + 1 < N_EXPERTS)
        def _():
            for cp in weight_copies(e + 1, 1 - ws):
                cp.start()

        rb = 128

        def cast(i, carry):
            rws = pl.ds(pl.multiple_of(i * rb, rb), rb)
            wg[rws, :] = stage_g[ws, rws, :].astype(BF16)
            wu[rws, :] = stage_u[ws, rws, :].astype(BF16)
            wd[rws, :] = stage_d[ws, rws, :].astype(BF16)
            return carry

        lax.fori_loop(0, D_MODEL // rb, cast, 0)

    wait_rows(slot)
    xs = xin.at[slot]
    x = jnp.concatenate([xs[pl.ds(j, n_rows, stride=N_CHUNK), :] for j in range(N_CHUNK)], axis=1).astype(BF16)
    hid = (jax.nn.silu(_dot(x, wg[...])) * _dot(x, wu[...])).astype(BF16)
    y = _dot(hid, wd[...])
    s2 = rows + ROW_PAD
    for n in range(out_tiles):
        for j in range(N_CHUNK):
            o_ref[0, 0, n, pl.ds(j * s2, rows), :] = y[n * rows:(n + 1) * rows, j * LANES:(j + 1) * LANES]
            o_ref[0, 0, n, pl.ds(j * s2 + rows, ROW_PAD), :] = jnp.zeros((ROW_PAD, LANES), F32)

    for r in range(n_rows):
        row_copy(idx_nxt2, r, ahead).start(priority=r % 2)

    @pl.when(step == n_steps - 1)
    def _():
        wait_rows(ahead)
        wait_rows((step + 1) % ROW_SLOTS)


def _moe_ffn(rows_glob, h2_tt, w_gate, w_up, w_down, layer, out_tiles, rows):
    n_b = rows_glob.shape[0] // N_EXPERTS
    n_rows = out_tiles * rows
    n_steps = N_EXPERTS * n_b
    s2 = rows + ROW_PAD
    wshape = w_gate.shape[2:]
    smem = functools.partial(pl.BlockSpec, memory_space=pltpu.SMEM)
    hbm = pl.BlockSpec(memory_space=pl.ANY)
    return pl.pallas_call(
        functools.partial(_moe_kernel, layer=layer, n_b=n_b, out_tiles=out_tiles, rows=rows),
        grid=(N_EXPERTS, n_b),
        in_specs=[
            smem((1, 1, n_rows), lambda e, b: (e * n_b + b, 0, 0)),
            smem((1, 1, n_rows), lambda e, b: (jnp.minimum(e * n_b + b + 1, n_steps - 1), 0, 0)),
            smem((1, 1, n_rows), lambda e, b: (jnp.minimum(e * n_b + b + 2, n_steps - 1), 0, 0)),
            hbm, hbm, hbm, hbm,
        ],
        out_specs=pl.BlockSpec((1, 1, out_tiles, N_CHUNK * s2, LANES), lambda e, b: (b, e, 0, 0, 0)),
        out_shape=jax.ShapeDtypeStruct((n_b, N_EXPERTS, out_tiles, N_CHUNK * s2, LANES), F32),
        scratch_shapes=[
            pltpu.VMEM((ROW_SLOTS, n_rows * N_CHUNK, LANES), F32),
            pltpu.VMEM((2,) + wshape, F32), pltpu.VMEM((2,) + wshape, F32), pltpu.VMEM((2,) + wshape, F32),
            pltpu.VMEM(wshape, BF16), pltpu.VMEM(wshape, BF16), pltpu.VMEM(wshape, BF16),
            pltpu.SemaphoreType.DMA((ROW_SLOTS,)), pltpu.SemaphoreType.DMA((2,)),
        ],
        compiler_params=_cparams("arbitrary", "arbitrary"),
        name="expert_ffn",
    )(rows_glob, rows_glob, rows_glob, h2_tt, w_gate, w_up, w_down)


def _global_rows(idx, bsz, s, tiles):
    cap = idx.shape[-1]
    glob = idx.reshape(bsz, N_EXPERTS, cap) + (jnp.arange(bsz, dtype=jnp.int32) * s)[:, None, None]
    return jnp.swapaxes(glob, 0, 1).reshape(N_EXPERTS * (bsz // tiles), 1, tiles * cap)


COMBINE_UNROLL = 8
COMBINE_EXPERTS = 4


def _combine_kernel(idx_ref, gate_ref, x_ref, y_ref, mod_ref, o_ref, acc, *, s, sp, cap, s2, n_seg, to_rows):
    eg = pl.program_id(1)
    base = pl.program_id(0) * cap if n_seg > 1 else 0

    @pl.when(eg == 0)
    def _():
        for j in range(N_CHUNK):
            acc[pl.ds(j * sp, s), :] = x_ref[0, j]
            acc[pl.ds(j * sp + s, ROW_PAD), :] = jnp.zeros((ROW_PAD, LANES), F32)

    g2 = mod_ref[0, 5]

    for k in range(COMBINE_EXPERTS):
        y = y_ref.at[0, k, 0]

        def body(i, carry):
            new = []
            for u in range(COMBINE_UNROLL):
                r = i * COMBINE_UNROLL + u
                t = idx_ref[0, 0, k * cap + r]
                yrow = y[pl.ds(base + r, N_CHUNK, stride=s2), :]
                cur = acc[pl.ds(t, N_CHUNK, stride=sp), :]
                new.append((t, cur + (gate_ref[0, 0, k * cap + r] * g2) * yrow))
            for t, val in new:
                acc[pl.ds(t, N_CHUNK, stride=sp), :] = val
            return carry

        lax.fori_loop(0, cap // COMBINE_UNROLL, body, 0)

    @pl.when(eg == N_EXPERTS // COMBINE_EXPERTS - 1)
    def _():
        for j in range(N_CHUNK):
            if to_rows:
                o_ref[0, :, j * LANES:(j + 1) * LANES] = acc[pl.ds(j * sp, s), :]
            else:
                o_ref[0, j] = acc[pl.ds(j * sp, s), :]


def _combine(idx, gate, x1cm, ycm, mod, s, to_rows):
    bsz = x1cm.shape[0]
    sp = s + ROW_PAD
    cap = idx.shape[-1]
    n_b, _, tiles, rows8, _ = ycm.shape
    s2 = rows8 // N_CHUNK
    n_seg = bsz // (n_b * tiles)
    eg = COMBINE_EXPERTS
    y_spec = pl.BlockSpec((1, eg, 1, rows8, LANES), lambda b, g: (b // tiles // n_seg, g, b % tiles, 0, 0))
    smem = functools.partial(pl.BlockSpec, memory_space=pltpu.SMEM)
    cm_spec = pl.BlockSpec((1, N_CHUNK, s, LANES), lambda b, g: (b, 0, 0, 0))
    if to_rows:
        out_spec = pl.BlockSpec((1, s, D_MODEL), lambda b, g: (b, 0, 0))
        out_shape = jax.ShapeDtypeStruct((bsz, s, D_MODEL), F32)
    else:
        out_spec = cm_spec
        out_shape = jax.ShapeDtypeStruct((bsz, N_CHUNK, s, LANES), F32)
    n_grp = N_EXPERTS // eg
    return pl.pallas_call(
        functools.partial(_combine_kernel, s=s, sp=sp, cap=cap, s2=s2, n_seg=n_seg, to_rows=to_rows),
        grid=(bsz, n_grp),
        in_specs=[
            smem((1, 1, eg * cap), lambda b, g: (b * n_grp + g, 0, 0)),
            smem((1, 1, eg * cap), lambda b, g: (b * n_grp + g, 0, 0)),
            cm_spec,
            y_spec,
            pl.BlockSpec((1, 6, N_CHUNK, LANES), lambda b, g: (b, 0, 0, 0)),
        ],
        out_specs=out_spec,
        out_shape=out_shape,
        scratch_shapes=[pltpu.VMEM((N_CHUNK * sp, LANES), F32)],
        compiler_params=_cparams("parallel", "arbitrary"),
        name="combine_rows" if to_rows else "combine_cm",
    )(idx.reshape(bsz * n_grp, 1, eg * cap), gate.reshape(bsz * n_grp, 1, eg * cap),
      x1cm, ycm, mod.reshape(bsz, 6, N_CHUNK, LANES))


def _rope_tables(n):
    rows = n // GRID_W
    row = jnp.repeat(jnp.arange(rows, dtype=F32), GRID_W)
    col = jnp.tile(jnp.arange(GRID_W, dtype=F32), rows)
    n_freq = HEAD_DIM // 4
    inv = jnp.power(ROPE_THETA, -jnp.arange(n_freq, dtype=F32) / n_freq)
    ang = jnp.concatenate([row[:, None] * inv, col[:, None] * inv], axis=-1)
    cos, sin = jnp.cos(ang), jnp.sin(ang)
    return jnp.concatenate([cos, cos], axis=-1), jnp.concatenate([-sin, sin], axis=-1)


def _router_rows(router_w):
    wt = router_w.T
    hi = wt.astype(BF16)
    lo = (wt - hi.astype(F32)).astype(BF16)
    return jnp.concatenate([hi, lo], axis=0)


def kernel(x, c, ctx, c_ctx, mod_w, mod_b, norm1_g, norm2_g, router_w, exp_w_gate, exp_w_up, exp_w_down,
           g_w_in, g_v_norm_g, g_spatial_w, g_spatial_b, g_w_out, att_w_in, att_q_norm_g, att_k_norm_g,
           att_w_out, p_w_in, p_w_group, p_scale):
    bsz, s, d = x.shape
    s_ctx = ctx.shape[1]
    depth = mod_w.shape[0]
    assert d == D_MODEL and bsz <= 8 and bsz % MOE_TILES == 0 and s % 512 == 0 and s_ctx % 128 == 0

    cc = jnp.zeros((16, d), F32).at[:bsz].set(c).at[bsz].set(c_ctx)
    mods = _modulation(cc, mod_w, mod_b).reshape(depth, 16, 6, d)

    x_cur, x_cm = x, False
    c_cur, c_cm = ctx, False
    for i in range(depth):
        kind, j = i % 3, i // 3
        update_ctx = any(l % 3 == 1 for l in range(i + 1, depth))
        last = i == depth - 1
        mod_lat = mods[i, :bsz]
        mod_ctx = jnp.broadcast_to(mods[i, bsz][None], (bsz, 6, d))
        g1 = norm1_g[i][None]
        g2 = norm2_g[i][None]
        rw2 = _router_rows(router_w[i])

        ctx_out = None
        if kind == 0:
            s_bias = jnp.repeat(g_spatial_b[j].T, LANES, axis=1)
            gm = functools.partial(
                _gmlp_layer, g1=g1, w_in=g_w_in[j].astype(BF16), v_g=g_v_norm_g[j][None],
                s_w=g_spatial_w[j].astype(BF16), s_bias=s_bias, w_out=g_w_out[j].astype(BF16),
                g2=g2, rw2=rw2)
            x1, h2, lg = gm(x_cur, x_cm, s, mod_lat)
            if update_ctx:
                ctx_out = gm(c_cur, c_cm, s_ctx, mod_ctx)
        elif kind == 1:
            assert not update_ctx
            cos, sin = _rope_tables(s)
            w_in = att_w_in[j].astype(BF16)
            qg, kg = att_q_norm_g[j][None], att_k_norm_g[j][None]
            q, k, v = _qkv_proj(x_cur, x_cm, s, mod_lat, g1, w_in, qg, kg, cos, sin, True)
            kc, vc = _qkv_proj(c_cur, c_cm, s_ctx, mod_ctx, g1, w_in, qg, kg,
                               jnp.ones((s_ctx, HEAD_DIM), F32), jnp.zeros((s_ctx, HEAD_DIM), F32), False)
            k_all = jnp.concatenate([kc, k], axis=1)
            v_all = jnp.concatenate([vc, v], axis=1)
            assert x_cm
            x1, h2, lg = _attn_layer(x_cur, s, q, k_all, v_all, mod_lat, att_w_out[j].astype(BF16), g2, rw2)
        else:
            assert not update_ctx and x_cm
            x1, h2, lg = _pool_layer(x_cur, s, mod_lat, g1, p_w_in[j].astype(BF16),
                                     p_w_group[j].astype(BF16), p_scale[j][None], g2, rw2)

        experts = functools.partial(_moe_ffn, w_gate=exp_w_gate, w_up=exp_w_up, w_down=exp_w_down, layer=i)
        idx, gate = _route(lg, s)
        ys = experts(_global_rows(idx, bsz, s, MOE_TILES), h2.reshape(bsz * s * N_CHUNK, LANES),
                     out_tiles=MOE_TILES, rows=idx.shape[-1])
        x_cur = _combine(idx, gate, x1, ys, mod_lat, s, last)
        x_cm = True
        if ctx_out is not None:
            c1, hc2, lgc = ctx_out
            idx_c, gate_c = _route(lgc, s_ctx)
            ys_c = experts(_global_rows(idx_c, bsz, s_ctx, bsz), hc2.reshape(bsz * s_ctx * N_CHUNK, LANES),
                           out_tiles=1, rows=bsz * idx_c.shape[-1])
            c_cur = _combine(idx_c, gate_c, c1, ys_c, mod_ctx, s_ctx, False)
            c_cm = True
    return x_cur
```

```python
import functools
import math

import jax
import jax.numpy as jnp
from jax import lax
from jax.experimental import pallas as pl
from jax.experimental.pallas import tpu as pltpu

F32 = jnp.float32
BF16 = jnp.bfloat16

D_MODEL = 1024
LANES = 128
SUBLANES = 8
N_CHUNK = D_MODEL // LANES
ROW_PAD = 8
EPS = 1e-6
GRID_W = 64
CHUNK = 128
A_GROUPS = 8
HEAD_DIM = 128
N_HEADS = 8
N_KV_HEADS = 2
Q_PER_KV = N_HEADS // N_KV_HEADS
ROPE_THETA = 10000.0
LOG2_E = 1.4426950408889634
GMLP_SUB_TILE = 512
ATTN_KEY_BLOCK = 768
POOL_WINDOWS = (2, 4, 8, 16)
POOL_GROUP_DIM = D_MODEL // len(POOL_WINDOWS)
POOL_HALO = 8
N_EXPERTS = 16
CAPACITY_FACTOR = 2
VMEM_LIMIT = 56 * 1024 * 1024


def _cparams(*sem):
    return pltpu.CompilerParams(dimension_semantics=sem, vmem_limit_bytes=VMEM_LIMIT)


def _dot(a, b):
    return jnp.dot(a, b, preferred_element_type=F32)


def _dot_nt(a, b):
    return lax.dot_general(a, b, (((1,), (1,)), ((), ())), preferred_element_type=F32)


def _chunks_to_tile(ref, lead, rows):
    return jnp.concatenate([ref[lead + (j, rows)] for j in range(N_CHUNK)], axis=-1)


def _tile_to_chunks(ref, lead, rows, val):
    for j in range(N_CHUNK):
        ref[lead + (j, rows)] = val[:, j * LANES:(j + 1) * LANES]


def _rms_mod(x, g, shift, scale):
    y = x * lax.rsqrt(jnp.mean(x * x, axis=-1, keepdims=True) + EPS)
    return (y * g) * (1.0 + scale) + shift


def _mod_kernel(c_ref, w_ref, b_ref, o_ref):
    a = jax.nn.silu(c_ref[...]).astype(BF16)
    o_ref[0] = _dot(a, w_ref[0].astype(BF16)) + b_ref[0]


def _modulation(cc, mod_w, mod_b):
    depth, d, n = mod_w.shape
    tn = 1536
    return pl.pallas_call(
        _mod_kernel,
        grid=(depth, n // tn),
        in_specs=[
            pl.BlockSpec((16, d), lambda l, j: (0, 0)),
            pl.BlockSpec((1, d, tn), lambda l, j: (l, 0, j)),
            pl.BlockSpec((1, 1, tn), lambda l, j: (l, 0, j)),
        ],
        out_specs=pl.BlockSpec((1, 16, tn), lambda l, j: (l, 0, j)),
        out_shape=jax.ShapeDtypeStruct((depth, 16, n), F32),
        compiler_params=_cparams("parallel", "parallel"),
        name="adaln_modulation",
    )(cc, mod_w, mod_b.reshape(depth, 1, n))


def _load_x(x_ref, x_cm, r0=0, n=None):
    rows = slice(None) if n is None else slice(r0, r0 + n)
    if x_cm:
        return _chunks_to_tile(x_ref, (0,), rows)
    return x_ref[0, rows]


def _epilogue(x, y, mod_ref, g2_ref, rw_ref, x1_ref, h2_ref, lg_ref, r0=0):
    n = x.shape[0]
    x1 = x + mod_ref[0, 2:3, :] * y
    _tile_to_chunks(x1_ref, (0,), slice(r0, r0 + n), x1)
    h2 = _rms_mod(x1, g2_ref[...], mod_ref[0, 3:4, :], mod_ref[0, 4:5, :])
    for j in range(N_CHUNK):
        h2_ref[0, pl.ds(r0 * N_CHUNK + j, n, stride=N_CHUNK), :] = h2[:, j * LANES:(j + 1) * LANES]
    hi = h2.astype(BF16)
    lo = (h2 - hi.astype(F32)).astype(BF16)
    rw = rw_ref[...]
    a = _dot_nt(rw, hi)
    b = _dot_nt(rw[:N_EXPERTS], lo)
    lg_ref[0, :, r0:r0 + n] = a[:N_EXPERTS] + a[N_EXPERTS:] + b


def _x_spec(x_cm, tm, d):
    if x_cm:
        return pl.BlockSpec((1, N_CHUNK, tm, LANES), lambda b, i: (b, 0, i, 0))
    return pl.BlockSpec((1, tm, d), lambda b, i: (b, i, 0))


def _const_spec(shape):
    nd = len(shape)
    return pl.BlockSpec(shape, lambda b, i: (0,) * nd)


def _epilogue_specs(bsz, s, tm):
    cm_spec = pl.BlockSpec((1, N_CHUNK, tm, LANES), lambda b, i: (b, 0, i, 0))
    tt_spec = pl.BlockSpec((1, tm * N_CHUNK, LANES), lambda b, i: (b, i, 0))
    out_specs = [cm_spec, tt_spec, pl.BlockSpec((1, N_EXPERTS, tm), lambda b, i: (b, 0, i))]
    out_shape = [
        jax.ShapeDtypeStruct((bsz, N_CHUNK, s, LANES), F32),
        jax.ShapeDtypeStruct((bsz, s * N_CHUNK, LANES), F32),
        jax.ShapeDtypeStruct((bsz, N_EXPERTS, s), F32),
    ]
    return out_specs, out_shape


def _mod_spec():
    return pl.BlockSpec((1, 6, D_MODEL), lambda b, i: (b, 0, 0))


def _gmlp_kernel(x_ref, mod_ref, g1_ref, win_ref, vg_ref, sw_ref, sb_ref, wout_ref, g2_ref, rw_ref,
                 x1_ref, h2_ref, lg_ref, *, x_cm, tm):
    sub = min(tm, GMLP_SUB_TILE)
    nc = sub // CHUNK
    for r0 in range(0, tm, sub):
        x = _load_x(x_ref, x_cm, r0, sub)
        h = _rms_mod(x, g1_ref[...], mod_ref[0, 0:1, :], mod_ref[0, 1:2, :]).astype(BF16)
        z = jax.nn.gelu(_dot(h, win_ref[...]))
        u = z[:, :D_MODEL]
        v = z[:, D_MODEL:]
        mu = jnp.mean(v, axis=-1, keepdims=True)
        vc = v - mu
        v = (vc * lax.rsqrt(jnp.mean(vc * vc, axis=-1, keepdims=True) + EPS)) * vg_ref[...]
        vb = v.astype(BF16)
        mixed = []
        for g in range(A_GROUPS):
            cols = jnp.concatenate(
                [vb[c * CHUNK:(c + 1) * CHUNK, g * LANES:(g + 1) * LANES] for c in range(nc)], axis=1)
            mixed.append(_dot(sw_ref[g], cols))
        sv = jnp.concatenate(
            [jnp.concatenate([mixed[g][:, c * LANES:(c + 1) * LANES] for g in range(A_GROUPS)], axis=1)
             + sb_ref[...] for c in range(nc)], axis=0)
        y = _dot((u * sv).astype(BF16), wout_ref[...])
        _epilogue(x, y, mod_ref, g2_ref, rw_ref, x1_ref, h2_ref, lg_ref, r0)


def _gmlp_layer(x, x_cm, s, mod, g1, w_in, v_g, s_w, s_bias, w_out, g2, rw2):
    bsz = x.shape[0]
    tm = min(1024, s)
    out_specs, out_shape = _epilogue_specs(bsz, s, tm)
    return pl.pallas_call(
        functools.partial(_gmlp_kernel, x_cm=x_cm, tm=tm),
        grid=(bsz, s // tm),
        in_specs=[
            _x_spec(x_cm, tm, D_MODEL), _mod_spec(), _const_spec((1, D_MODEL)),
            _const_spec(w_in.shape), _const_spec((1, D_MODEL)), _const_spec(s_w.shape),
            _const_spec(s_bias.shape), _const_spec(w_out.shape), _const_spec((1, D_MODEL)),
            _const_spec(rw2.shape),
        ],
        out_specs=out_specs,
        out_shape=out_shape,
        compiler_params=_cparams("parallel", "arbitrary"),
        name="gmlp_mixer",
    )(x, mod, g1, w_in, v_g, s_w, s_bias, w_out, g2, rw2)


def _qkv_kernel(x_ref, mod_ref, g1_ref, w_ref, qg_ref, kg_ref, cos_ref, sin_ref, *out_refs,
                x_cm, want_q):
    x = _load_x(x_ref, x_cm)
    h = _rms_mod(x, g1_ref[...], mod_ref[0, 0:1, :], mod_ref[0, 1:2, :]).astype(BF16)
    qkv = _dot(h, w_ref[...])
    cos = cos_ref[...]
    sin = sin_ref[...]

    def norm_rope(t, g):
        t = (t * lax.rsqrt(jnp.mean(t * t, axis=-1, keepdims=True) + EPS)) * g
        return t * cos + pltpu.roll(t, HEAD_DIM // 2, 1) * sin

    qd = N_HEADS * HEAD_DIM
    kvd = N_KV_HEADS * HEAD_DIM
    if want_q:
        q_ref, k_ref, v_ref = out_refs
        for hd in range(N_HEADS):
            qh = norm_rope(qkv[:, hd * HEAD_DIM:(hd + 1) * HEAD_DIM], qg_ref[...])
            q_ref[0, :, hd * HEAD_DIM:(hd + 1) * HEAD_DIM] = (qh * (HEAD_DIM ** -0.5 * LOG2_E)).astype(BF16)
    else:
        k_ref, v_ref = out_refs
    for hd in range(N_KV_HEADS):
        kh = norm_rope(qkv[:, qd + hd * HEAD_DIM:qd + (hd + 1) * HEAD_DIM], kg_ref[...])
        k_ref[0, :, hd * HEAD_DIM:(hd + 1) * HEAD_DIM] = kh.astype(BF16)
    v_ref[0] = qkv[:, qd + kvd:].astype(BF16)


def _qkv_proj(x, x_cm, s, mod, g1, w_in, q_g, k_g, cos, sin, want_q):
    bsz = x.shape[0]
    tm = min(512, s)
    qd = N_HEADS * HEAD_DIM
    kvd = N_KV_HEADS * HEAD_DIM
    kv_spec = pl.BlockSpec((1, tm, kvd), lambda b, i: (b, i, 0))
    kv_shape = jax.ShapeDtypeStruct((bsz, s, kvd), BF16)
    out_specs = [kv_spec, kv_spec]
    out_shape = [kv_shape, kv_shape]
    if want_q:
        out_specs = [pl.BlockSpec((1, tm, qd), lambda b, i: (b, i, 0))] + out_specs
        out_shape = [jax.ShapeDtypeStruct((bsz, s, qd), BF16)] + out_shape
    return pl.pallas_call(
        functools.partial(_qkv_kernel, x_cm=x_cm, want_q=want_q),
        grid=(bsz, s // tm),
        in_specs=[
            _x_spec(x_cm, tm, D_MODEL), _mod_spec(), _const_spec((1, D_MODEL)),
            _const_spec(w_in.shape), _const_spec((1, HEAD_DIM)), _const_spec((1, HEAD_DIM)),
            pl.BlockSpec((tm, HEAD_DIM), lambda b, i: (i, 0)),
            pl.BlockSpec((tm, HEAD_DIM), lambda b, i: (i, 0)),
        ],
        out_specs=out_specs,
        out_shape=out_shape,
        compiler_params=_cparams("parallel", "arbitrary"),
        name="qkv_proj",
    )(x, mod, g1, w_in, q_g, k_g, cos, sin)


def _attn_kernel(x_ref, q_ref, k_ref, v_ref, mod_ref, wout_ref, g2_ref, rw_ref,
                 x1_ref, h2_ref, lg_ref, *, tq):
    x = _load_x(x_ref, True)
    heads = [None] * N_HEADS
    sk = k_ref.shape[1]
    ones_col = jnp.where(lax.broadcasted_iota(jnp.int32, (sk, HEAD_DIM), 1) == 0, 1.0, 0.0).astype(BF16)
    for kh in range(N_KV_HEADS):
        k = k_ref[0, :, kh * HEAD_DIM:(kh + 1) * HEAD_DIM]
        v = jnp.concatenate([v_ref[0, :, kh * HEAD_DIM:(kh + 1) * HEAD_DIM], ones_col], axis=1)
        qg = jnp.concatenate(
            [q_ref[0, :, (kh * Q_PER_KV + g) * HEAD_DIM:(kh * Q_PER_KV + g + 1) * HEAD_DIM]
             for g in range(Q_PER_KV)], axis=0)
        m = jnp.full((Q_PER_KV * tq, 1), -jnp.inf, F32)
        o = jnp.zeros((Q_PER_KV * tq, 2 * HEAD_DIM), F32)
        for kb in range(0, sk, ATTN_KEY_BLOCK):
            sc = _dot_nt(qg, k[kb:kb + ATTN_KEY_BLOCK])
            m_new = jnp.maximum(m, jnp.max(sc, axis=-1, keepdims=True))
            p = jnp.exp2(sc - m_new).astype(BF16)
            o = o * jnp.exp2(m - m_new) + _dot(p, v[kb:kb + ATTN_KEY_BLOCK])
            m = m_new
        o = o[:, :HEAD_DIM] / o[:, HEAD_DIM:HEAD_DIM + 1]
        for g in range(Q_PER_KV):
            heads[kh * Q_PER_KV + g] = o[g * tq:(g + 1) * tq].astype(BF16)
    y = _dot(jnp.concatenate(heads, axis=1), wout_ref[...])
    _epilogue(x, y, mod_ref, g2_ref, rw_ref, x1_ref, h2_ref, lg_ref)


def _attn_layer(x, s, q, k_all, v_all, mod, w_out, g2, rw2):
    bsz = x.shape[0]
    tq = 512
    sk = k_all.shape[1]
    qd = N_HEADS * HEAD_DIM
    kvd = N_KV_HEADS * HEAD_DIM
    out_specs, out_shape = _epilogue_specs(bsz, s, tq)
    return pl.pallas_call(
        functools.partial(_attn_kernel, tq=tq),
        grid=(bsz, s // tq),
        in_specs=[
            _x_spec(True, tq, D_MODEL),
            pl.BlockSpec((1, tq, qd), lambda b, i: (b, i, 0)),
            pl.BlockSpec((1, sk, kvd), lambda b, i: (b, 0, 0)),
            pl.BlockSpec((1, sk, kvd), lambda b, i: (b, 0, 0)),
            _mod_spec(), _const_spec(w_out.shape), _const_spec((1, D_MODEL)), _const_spec(rw2.shape),
        ],
        out_specs=out_specs,
        out_shape=out_shape,
        compiler_params=_cparams("parallel", "arbitrary"),
        name="gqa_attention",
    )(x, q, k_all, v_all, mod, w_out, g2, rw2)


def _pool_kernel(xp_ref, x_ref, xn_ref, mod_ref, g1_ref, win_ref, band_ref, wgrp_ref, ps_ref, g2_ref,
                 rw_ref, x1_ref, h2_ref, lg_ref, *, tm, s):
    i = pl.program_id(1)
    x = _load_x(x_ref, True)
    xe = jnp.concatenate([_load_x(xp_ref, True), x, _load_x(xn_ref, True)], axis=0)
    h = _rms_mod(xe, g1_ref[...], mod_ref[0, 0:1, :], mod_ref[0, 1:2, :]).astype(BF16)
    z = _dot(h, win_ref[...])
    te = band_ref.shape[2]
    zp = z[:POOL_HALO] * jnp.where(i > 0, 1.0, 0.0)
    zn = z[POOL_HALO + tm:] * jnp.where(i < pl.num_programs(1) - 1, 1.0, 0.0)
    zc = z[POOL_HALO:POOL_HALO + tm]
    zb = jnp.concatenate([zp, zc, zn, jnp.zeros((te - tm - 2 * POOL_HALO, D_MODEL), F32)], axis=0).astype(BF16)
    tcol = i * tm + lax.broadcasted_iota(jnp.int32, (tm, 1), 0)
    outs = []
    for gi, w in enumerate(POOL_WINDOWS):
        half = w // 2
        cnt = (jnp.minimum(tcol + half - 1, s - 1) - jnp.maximum(tcol - half, 0) + 1).astype(F32)
        lo, hi_ = gi * POOL_GROUP_DIM, (gi + 1) * POOL_GROUP_DIM
        pooled = _dot(band_ref[gi], zb[:, lo:hi_]) / cnt - zc[:, lo:hi_]
        outs.append(_dot(pooled.astype(BF16), wgrp_ref[gi]))
    y = jnp.concatenate(outs, axis=1) * ps_ref[...]
    _epilogue(x, y, mod_ref, g2_ref, rw_ref, x1_ref, h2_ref, lg_ref)


def _pool_bands(tm):
    te = tm + LANES
    t = jnp.arange(tm)[:, None]
    off = jnp.arange(te)[None, :] - POOL_HALO - t
    inside = jnp.arange(te)[None, :] < tm + 2 * POOL_HALO
    return jnp.stack([(inside & (off >= -(w // 2)) & (off <= w // 2 - 1)) for w in POOL_WINDOWS]).astype(BF16)


def _pool_layer(x, s, mod, g1, w_in, w_grp, p_scale, g2, rw2):
    bsz = x.shape[0]
    tm = 512
    nb = tm // POOL_HALO
    last = s // POOL_HALO - 1
    out_specs, out_shape = _epilogue_specs(bsz, s, tm)
    halo = (1, N_CHUNK, POOL_HALO, LANES)
    bands = _pool_bands(tm)
    return pl.pallas_call(
        functools.partial(_pool_kernel, tm=tm, s=s),
        grid=(bsz, s // tm),
        in_specs=[
            pl.BlockSpec(halo, lambda b, i: (b, 0, jnp.maximum(i * nb - 1, 0), 0)),
            _x_spec(True, tm, D_MODEL),
            pl.BlockSpec(halo, lambda b, i: (b, 0, jnp.minimum((i + 1) * nb, last), 0)),
            _mod_spec(), _const_spec((1, D_MODEL)), _const_spec(w_in.shape), _const_spec(bands.shape),
            _const_spec(w_grp.shape), _const_spec((1, D_MODEL)), _const_spec((1, D_MODEL)),
            _const_spec(rw2.shape),
        ],
        out_specs=out_specs,
        out_shape=out_shape,
        compiler_params=_cparams("parallel", "arbitrary"),
        name="pool_mixer",
    )(x, x, x, mod, g1, w_in, bands, w_grp, p_scale, g2, rw2)


SORT_PHASE_BITS = 4


def _comes_first(a, ia, b, ib):
    return (a > b) | ((a == b) & (ia < ib))


def _sort_tokens(key_ref, tok_ref, sub, n):
    nv = key_ref.shape[0]
    vbits = nv.bit_length() - 1

    def exchange(ka, ta, kb, tb, inv):
        keep = _comes_first(ka, ta, kb, tb)
        if inv is not None:
            keep = jnp.logical_xor(keep, inv)
        return jnp.where(keep, ka, kb), jnp.where(keep, ta, tb), jnp.where(keep, kb, ka), jnp.where(keep, tb, ta)

    k = 2
    while k <= n:
        dists = [k >> d for d in range(1, k.bit_length())]
        sub_inv = None if k >= n else ((sub & (k >> vbits)) != 0 if k >= nv else False)

        for j in [d for d in dists if d >= nv]:
            js = j >> vbits
            lower = (sub & js) == 0
            later = jnp.logical_not(lower) if sub_inv is None else jnp.logical_xor(lower, jnp.logical_not(sub_inv))

            def rotate_stage(g, carry, js=js, lower=lower, later=later):
                for u in range(SUBLANES):
                    v = g * SUBLANES + u
                    x, t = key_ref[v], tok_ref[v]
                    px = jnp.where(lower, pltpu.roll(x, SUBLANES - js, 0), pltpu.roll(x, js, 0))
                    pt = jnp.where(lower, pltpu.roll(t, SUBLANES - js, 0), pltpu.roll(t, js, 0))
                    keep = jnp.logical_xor(_comes_first(x, t, px, pt), later)
                    key_ref[v] = jnp.where(keep, x, px)
                    tok_ref[v] = jnp.where(keep, t, pt)
                return carry

            lax.fori_loop(0, nv // SUBLANES, rotate_stage, 0)

        vreg_dists = [d for d in dists if d < nv]
        for p0 in range(0, len(vreg_dists), SORT_PHASE_BITS):
            phase = vreg_dists[p0:p0 + SORT_PHASE_BITS]
            lo = phase[-1].bit_length() - 1
            nb = len(phase)
            members = [m << lo for m in range(1 << nb)]

            def group_phase(q, carry, phase=phase, lo=lo, nb=nb, members=members, k=k, sub_inv=sub_inv):
                base = ((q >> lo) << (lo + nb)) | (q & ((1 << lo) - 1))
                if sub_inv is False:
                    bit = (base >> (k.bit_length() - 1)) & 1
                    inv = lax.broadcast(bit, (SUBLANES, LANES)) != 0
                else:
                    inv = sub_inv
                ks = {m: key_ref[base + m] for m in members}
                ts = {m: tok_ref[base + m] for m in members}
                for j in phase:
                    for m in members:
                        if not m & j:
                            ks[m], ts[m], ks[m | j], ts[m | j] = exchange(ks[m], ts[m], ks[m | j], ts[m | j], inv)
                for m in members:
                    key_ref[base + m] = ks[m]
                    tok_ref[base + m] = ts[m]
                return carry

            lax.fori_loop(0, nv >> nb, group_phase, 0)
        k *= 2


def _route_kernel(lg_ref, idx_ref, gate_ref, key_ref, tok_ref, *, s, cap):
    lg = lg_ref[...]
    ex = jnp.exp(lg - jnp.max(lg, axis=1, keepdims=True))
    aff = ex / jnp.sum(ex, axis=1, keepdims=True)
    nv = s // SUBLANES
    key_ref[...] = jnp.transpose(aff.reshape(LANES, s)).reshape(nv, SUBLANES, LANES)
    tok_ref[...] = (lax.broadcasted_iota(jnp.int32, (nv, SUBLANES, LANES), 0) * SUBLANES
                    + lax.broadcasted_iota(jnp.int32, (nv, SUBLANES, LANES), 1))
    sub = lax.broadcasted_iota(jnp.int32, (SUBLANES, LANES), 0)
    _sort_tokens(key_ref, tok_ref, sub, s)
    top_k = key_ref[:, 0, :]
    top_i = tok_ref[:, 0, :].astype(F32)
    pad = LANES - cap % LANES if cap % LANES else 0
    if pad:
        top_k = jnp.concatenate([top_k, jnp.zeros((pad, LANES), F32)], axis=0)
        top_i = jnp.concatenate([top_i, jnp.zeros((pad, LANES), F32)], axis=0)
    gate_ref[...] = jnp.transpose(top_k)[:, :cap]
    idx_ref[...] = jnp.transpose(top_i)[:, :cap].astype(jnp.int32)


def _route(logits_t, s):
    bsz = logits_t.shape[0]
    cap = CAPACITY_FACTOR * s // N_EXPERTS
    n_rows = bsz * N_EXPERTS
    assert n_rows == LANES and cap == s // SUBLANES
    return pl.pallas_call(
        functools.partial(_route_kernel, s=s, cap=cap),
        out_shape=[
            jax.ShapeDtypeStruct((n_rows, cap), jnp.int32),
            jax.ShapeDtypeStruct((n_rows, cap), F32),
        ],
        scratch_shapes=[pltpu.VMEM((s // SUBLANES, SUBLANES, LANES), F32),
                        pltpu.VMEM((s // SUBLANES, SUBLANES, LANES), jnp.int32)],
        compiler_params=pltpu.CompilerParams(vmem_limit_bytes=VMEM_LIMIT),
        name="expert_choice_route",
    )(logits_t)


MOE_TILES = 2
ROW_SLOTS = 3


def _moe_kernel(idx_cur, idx_nxt, idx_nxt2, h2_hbm, wg_hbm, wu_hbm, wd_hbm, o_ref,
                xin, stage_g, stage_u, stage_d, wg, wu, wd, sem_x, sem_w,
                *, layer, n_b, out_tiles, rows):
    e = pl.program_id(0)
    b = pl.program_id(1)
    step = e * n_b + b
    n_steps = N_EXPERTS * n_b
    n_rows = out_tiles * rows
    slot = step % ROW_SLOTS
    ahead = (step + ROW_SLOTS - 1) % ROW_SLOTS

    def row_copy(idx_ref, r, dst_slot):
        src_row = pl.multiple_of(idx_ref[0, 0, r] * N_CHUNK, N_CHUNK)
        return pltpu.make_async_copy(h2_hbm.at[pl.ds(src_row, N_CHUNK), :],
                                     xin.at[dst_slot, pl.ds(r * N_CHUNK, N_CHUNK), :],
                                     sem_x.at[dst_slot])

    def wait_rows(s_):
        pltpu.make_async_copy(h2_hbm.at[pl.ds(0, n_rows * N_CHUNK), :], xin.at[s_], sem_x.at[s_]).wait()

    def weight_copies(expert, ws):
        return [pltpu.make_async_copy(w.at[layer, expert], st.at[ws], sem_w.at[ws])
                for w, st in ((wg_hbm, stage_g), (wu_hbm, stage_u), (wd_hbm, stage_d))]

    @pl.when(step == 0)
    def _():
        for cp in weight_copies(0, 0):
            cp.start()

        def issue(r, carry):
            row_copy(idx_cur, r, 0).start()
            row_copy(idx_nxt, r, 1).start()
            return carry

        lax.fori_loop(0, n_rows, issue, 0)

    @pl.when(b == 0)
    def _():
        ws = e % 2
        for cp in weight_copies(e, ws):
            cp.wait()

        @pl.when(e + 1 < N_EXPERTS)
        def _():
            for cp in weight_copies(e + 1, 1 - ws):
                cp.start()

        rb = 128

        def cast(i, carry):
            rws = pl.ds(pl.multiple_of(i * rb, rb), rb)
            wg[rws, :] = stage_g[ws, rws, :].astype(BF16)
            wu[rws, :] = stage_u[ws, rws, :].astype(BF16)
            wd[rws, :] = stage_d[ws, rws, :].astype(BF16)
            return carry

        lax.fori_loop(0, D_MODEL // rb, cast, 0)

    wait_rows(slot)
    xs = xin.at[slot]
    x = jnp.concatenate([xs[pl.ds(j, n_rows, stride=N_CHUNK), :] for j in range(N_CHUNK)], axis=1).astype(BF16)
    hid = (jax.nn.silu(_dot(x, wg[...])) * _dot(x, wu[...])).astype(BF16)
    y = _dot(hid, wd[...])
    s2 = rows + ROW_PAD
    for n in range(out_tiles):
        for j in range(N_CHUNK):
            o_ref[0, 0, n, pl.ds(j * s2, rows), :] = y[n * rows:(n + 1) * rows, j * LANES:(j + 1) * LANES]
            o_ref[0, 0, n, pl.ds(j * s2 + rows, ROW_PAD), :] = jnp.zeros((ROW_PAD, LANES), F32)

    for r in range(n_rows):
        row_copy(idx_nxt2, r, ahead).start(priority=r % 2)

    @pl.when(step == n_steps - 1)
    def _():
        wait_rows(ahead)
        wait_rows((step + 1) % ROW_SLOTS)


def _moe_ffn(rows_glob, h2_tt, w_gate, w_up, w_down, layer, out_tiles, rows):
    n_b = rows_glob.shape[0] // N_EXPERTS
    n_rows = out_tiles * rows
    n_steps = N_EXPERTS * n_b
    s2 = rows + ROW_PAD
    wshape = w_gate.shape[2:]
    smem = functools.partial(pl.BlockSpec, memory_space=pltpu.SMEM)
    hbm = pl.BlockSpec(memory_space=pl.ANY)
    return pl.pallas_call(
        functools.partial(_moe_kernel, layer=layer, n_b=n_b, out_tiles=out_tiles, rows=rows),
        grid=(N_EXPERTS, n_b),
        in_specs=[
            smem((1, 1, n_rows), lambda e, b: (e * n_b + b, 0, 0)),
            smem((1, 1, n_rows), lambda e, b: (jnp.minimum(e * n_b + b + 1, n_steps - 1), 0, 0)),
            smem((1, 1, n_rows), lambda e, b: (jnp.minimum(e * n_b + b + 2, n_steps - 1), 0, 0)),
            hbm, hbm, hbm, hbm,
        ],
        out_specs=pl.BlockSpec((1, 1, out_tiles, N_CHUNK * s2, LANES), lambda e, b: (b, e, 0, 0, 0)),
        out_shape=jax.ShapeDtypeStruct((n_b, N_EXPERTS, out_tiles, N_CHUNK * s2, LANES), F32),
        scratch_shapes=[
            pltpu.VMEM((ROW_SLOTS, n_rows * N_CHUNK, LANES), F32),
            pltpu.VMEM((2,) + wshape, F32), pltpu.VMEM((2,) + wshape, F32), pltpu.VMEM((2,) + wshape, F32),
            pltpu.VMEM(wshape, BF16), pltpu.VMEM(wshape, BF16), pltpu.VMEM(wshape, BF16),
            pltpu.SemaphoreType.DMA((ROW_SLOTS,)), pltpu.SemaphoreType.DMA((2,)),
        ],
        compiler_params=_cparams("arbitrary", "arbitrary"),
        name="expert_ffn",
    )(rows_glob, rows_glob, rows_glob, h2_tt, w_gate, w_up, w_down)


def _global_rows(idx, bsz, s, tiles):
    cap = idx.shape[-1]
    glob = idx.reshape(bsz, N_EXPERTS, cap) + (jnp.arange(bsz, dtype=jnp.int32) * s)[:, None, None]
    return jnp.swapaxes(glob, 0, 1).reshape(N_EXPERTS * (bsz // tiles), 1, tiles * cap)


COMBINE_UNROLL = 8
COMBINE_EXPERTS = 4


def _combine_kernel(idx_ref, gate_ref, x_ref, y_ref, mod_ref, o_ref, acc, *, s, sp, cap, s2, n_seg, to_rows):
    eg = pl.program_id(1)
    base = pl.program_id(0) * cap if n_seg > 1 else 0

    @pl.when(eg == 0)
    def _():
        for j in range(N_CHUNK):
            acc[pl.ds(j * sp, s), :] = x_ref[0, j]
            acc[pl.ds(j * sp + s, ROW_PAD), :] = jnp.zeros((ROW_PAD, LANES), F32)

    g2 = mod_ref[0, 5]

    for k in range(COMBINE_EXPERTS):
        y = y_ref.at[0, k, 0]

        def body(i, carry):
            new = []
            for u in range(COMBINE_UNROLL):
                r = i * COMBINE_UNROLL + u
                t = idx_ref[0, 0, k * cap + r]
                yrow = y[pl.ds(base + r, N_CHUNK, stride=s2), :]
                cur = acc[pl.ds(t, N_CHUNK, stride=sp), :]
                new.append((t, cur + (gate_ref[0, 0, k * cap + r] * g2) * yrow))
            for t, val in new:
                acc[pl.ds(t, N_CHUNK, stride=sp), :] = val
            return carry

        lax.fori_loop(0, cap // COMBINE_UNROLL, body, 0)

    @pl.when(eg == N_EXPERTS // COMBINE_EXPERTS - 1)
    def _():
        for j in range(N_CHUNK):
            if to_rows:
                o_ref[0, :, j * LANES:(j + 1) * LANES] = acc[pl.ds(j * sp, s), :]
            else:
                o_ref[0, j] = acc[pl.ds(j * sp, s), :]


def _combine(idx, gate, x1cm, ycm, mod, s, to_rows):
    bsz = x1cm.shape[0]
    sp = s + ROW_PAD
    cap = idx.shape[-1]
    n_b, _, tiles, rows8, _ = ycm.shape
    s2 = rows8 // N_CHUNK
    n_seg = bsz // (n_b * tiles)
    eg = COMBINE_EXPERTS
    y_spec = pl.BlockSpec((1, eg, 1, rows8, LANES), lambda b, g: (b // tiles // n_seg, g, b % tiles, 0, 0))
    smem = functools.partial(pl.BlockSpec, memory_space=pltpu.SMEM)
    cm_spec = pl.BlockSpec((1, N_CHUNK, s, LANES), lambda b, g: (b, 0, 0, 0))
    if to_rows:
        out_spec = pl.BlockSpec((1, s, D_MODEL), lambda b, g: (b, 0, 0))
        out_shape = jax.ShapeDtypeStruct((bsz, s, D_MODEL), F32)
    else:
        out_spec = cm_spec
        out_shape = jax.ShapeDtypeStruct((bsz, N_CHUNK, s, LANES), F32)
    n_grp = N_EXPERTS // eg
    return pl.pallas_call(
        functools.partial(_combine_kernel, s=s, sp=sp, cap=cap, s2=s2, n_seg=n_seg, to_rows=to_rows),
        grid=(bsz, n_grp),
        in_specs=[
            smem((1, 1, eg * cap), lambda b, g: (b * n_grp + g, 0, 0)),
            smem((1, 1, eg * cap), lambda b, g: (b * n_grp + g, 0, 0)),
            cm_spec,
            y_spec,
            pl.BlockSpec((1, 6, N_CHUNK, LANES), lambda b, g: (b, 0, 0, 0)),
        ],
        out_specs=out_spec,
        out_shape=out_shape,
        scratch_shapes=[pltpu.VMEM((N_CHUNK * sp, LANES), F32)],
        compiler_params=_cparams("parallel", "arbitrary"),
        name="combine_rows" if to_rows else "combine_cm",
    )(idx.reshape(bsz * n_grp, 1, eg * cap), gate.reshape(bsz * n_grp, 1, eg * cap),
      x1cm, ycm, mod.reshape(bsz, 6, N_CHUNK, LANES))


def _rope_tables(n):
    rows = n // GRID_W
    row = jnp.repeat(jnp.arange(rows, dtype=F32), GRID_W)
    col = jnp.tile(jnp.arange(GRID_W, dtype=F32), rows)
    n_freq = HEAD_DIM // 4
    inv = jnp.power(ROPE_THETA, -jnp.arange(n_freq, dtype=F32) / n_freq)
    ang = jnp.concatenate([row[:, None] * inv, col[:, None] * inv], axis=-1)
    cos, sin = jnp.cos(ang), jnp.sin(ang)
    return jnp.concatenate([cos, cos], axis=-1), jnp.concatenate([-sin, sin], axis=-1)


def _router_rows(router_w):
    wt = router_w.T
    hi = wt.astype(BF16)
    lo = (wt - hi.astype(F32)).astype(BF16)
    return jnp.concatenate([hi, lo], axis=0)


def kernel(x, c, ctx, c_ctx, mod_w, mod_b, norm1_g, norm2_g, router_w, exp_w_gate, exp_w_up, exp_w_down,
           g_w_in, g_v_norm_g, g_spatial_w, g_spatial_b, g_w_out, att_w_in, att_q_norm_g, att_k_norm_g,
           att_w_out, p_w_in, p_w_group, p_scale):
    bsz, s, d = x.shape
    s_ctx = ctx.shape[1]
    depth = mod_w.shape[0]
    assert d == D_MODEL and bsz <= 8 and bsz % MOE_TILES == 0 and s % 512 == 0 and s_ctx % 128 == 0

    cc = jnp.zeros((16, d), F32).at[:bsz].set(c).at[bsz].set(c_ctx)
    mods = _modulation(cc, mod_w, mod_b).reshape(depth, 16, 6, d)

    x_cur, x_cm = x, False
    c_cur, c_cm = ctx, False
    for i in range(depth):
        kind, j = i % 3, i // 3
        update_ctx = any(l % 3 == 1 for l in range(i + 1, depth))
        last = i == depth - 1
        mod_lat = mods[i, :bsz]
        mod_ctx = jnp.broadcast_to(mods[i, bsz][None], (bsz, 6, d))
        g1 = norm1_g[i][None]
        g2 = norm2_g[i][None]
        rw2 = _router_rows(router_w[i])

        ctx_out = None
        if kind == 0:
            s_bias = jnp.repeat(g_spatial_b[j].T, LANES, axis=1)
            gm = functools.partial(
                _gmlp_layer, g1=g1, w_in=g_w_in[j].astype(BF16), v_g=g_v_norm_g[j][None],
                s_w=g_spatial_w[j].astype(BF16), s_bias=s_bias, w_out=g_w_out[j].astype(BF16),
                g2=g2, rw2=rw2)
            x1, h2, lg = gm(x_cur, x_cm, s, mod_lat)
            if update_ctx:
                ctx_out = gm(c_cur, c_cm, s_ctx, mod_ctx)
        elif kind == 1:
            assert not update_ctx
            cos, sin = _rope_tables(s)
            w_in = att_w_in[j].astype(BF16)
            qg, kg = att_q_norm_g[j][None], att_k_norm_g[j][None]
            q, k, v = _qkv_proj(x_cur, x_cm, s, mod_lat, g1, w_in, qg, kg, cos, sin, True)
            kc, vc = _qkv_proj(c_cur, c_cm, s_ctx, mod_ctx, g1, w_in, qg, kg,
                               jnp.ones((s_ctx, HEAD_DIM), F32), jnp.zeros((s_ctx, HEAD_DIM), F32), False)
            k_all = jnp.concatenate([kc, k], axis=1)
            v_all = jnp.concatenate([vc, v], axis=1)
            assert x_cm
            x1, h2, lg = _attn_layer(x_cur, s, q, k_all, v_all, mod_lat, att_w_out[j].astype(BF16), g2, rw2)
        else:
            assert not update_ctx and x_cm
            x1, h2, lg = _pool_layer(x_cur, s, mod_lat, g1, p_w_in[j].astype(BF16),
                                     p_w_group[j].astype(BF16), p_scale[j][None], g2, rw2)

        experts = functools.partial(_moe_ffn, w_gate=exp_w_gate, w_up=exp_w_up, w_down=exp_w_down, layer=i)
        idx, gate = _route(lg, s)
        ys = experts(_global_rows(idx, bsz, s, MOE_TILES), h2.reshape(bsz * s * N_CHUNK, LANES),
                     out_tiles=MOE_TILES, rows=idx.shape[-1])
        x_cur = _combine(idx, gate, x1, ys, mod_lat, s, last)
        x_cm = True
        if ctx_out is not None:
            c1, hc2, lgc = ctx_out
            idx_c, gate_c = _route(lgc, s_ctx)
            ys_c = experts(_global_rows(idx_c, bsz, s_ctx, bsz), hc2.reshape(bsz * s_ctx * N_CHUNK, LANES),
                           out_tiles=1, rows=bsz * idx_c.shape[-1])
            c_cur = _combine(idx_c, gate_c, c1, ys_c, mod_ctx, s_ctx, False)
            c_cm = True
    return x_cur
```

```python
import functools
import math

import jax
import jax.numpy as jnp
from jax import lax
from jax.experimental import pallas as pl
from jax.experimental.pallas import tpu as pltpu

F32 = jnp.float32
BF16 = jnp.bfloat16

D_MODEL = 1024
LANES = 128
SUBLANES = 8
N_CHUNK = D_MODEL // LANES
ROW_PAD = 8
EPS = 1e-6
GRID_W = 64
CHUNK = 128
A_GROUPS = 8
HEAD_DIM = 128
N_HEADS = 8
N_KV_HEADS = 2
Q_PER_KV = N_HEADS // N_KV_HEADS
ROPE_THETA = 10000.0
LOG2_E = 1.4426950408889634
GMLP_SUB_TILE = 512
ATTN_KEY_BLOCK = 768
POOL_WINDOWS = (2, 4, 8, 16)
POOL_GROUP_DIM = D_MODEL // len(POOL_WINDOWS)
POOL_HALO = 8
N_EXPERTS = 16
CAPACITY_FACTOR = 2
VMEM_LIMIT = 56 * 1024 * 1024


def _cparams(*sem):
    return pltpu.CompilerParams(dimension_semantics=sem, vmem_limit_bytes=VMEM_LIMIT)


def _dot(a, b):
    return jnp.dot(a, b, preferred_element_type=F32)


def _dot_nt(a, b):
    return lax.dot_general(a, b, (((1,), (1,)), ((), ())), preferred_element_type=F32)


def _chunks_to_tile(ref, lead, rows):
    return jnp.concatenate([ref[lead + (j, rows)] for j in range(N_CHUNK)], axis=-1)


def _tile_to_chunks(ref, lead, rows, val):
    for j in range(N_CHUNK):
        ref[lead + (j, rows)] = val[:, j * LANES:(j + 1) * LANES]


def _rms_mod(x, g, shift, scale):
    y = x * lax.rsqrt(jnp.mean(x * x, axis=-1, keepdims=True) + EPS)
    return (y * g) * (1.0 + scale) + shift


def _mod_kernel(c_ref, w_ref, b_ref, o_ref):
    a = jax.nn.silu(c_ref[...]).astype(BF16)
    o_ref[0] = _dot(a, w_ref[0].astype(BF16)) + b_ref[0]


def _modulation(cc, mod_w, mod_b):
    depth, d, n = mod_w.shape
    tn = 1536
    return pl.pallas_call(
        _mod_kernel,
        grid=(depth, n // tn),
        in_specs=[
            pl.BlockSpec((16, d), lambda l, j: (0, 0)),
            pl.BlockSpec((1, d, tn), lambda l, j: (l, 0, j)),
            pl.BlockSpec((1, 1, tn), lambda l, j: (l, 0, j)),
        ],
        out_specs=pl.BlockSpec((1, 16, tn), lambda l, j: (l, 0, j)),
        out_shape=jax.ShapeDtypeStruct((depth, 16, n), F32),
        compiler_params=_cparams("parallel", "parallel"),
        name="adaln_modulation",
    )(cc, mod_w, mod_b.reshape(depth, 1, n))


def _load_x(x_ref, x_cm, r0=0, n=None):
    rows = slice(None) if n is None else slice(r0, r0 + n)
    if x_cm:
        return _chunks_to_tile(x_ref, (0,), rows)
    return x_ref[0, rows]


def _epilogue(x, y, mod_ref, g2_ref, rw_ref, x1_ref, h2_ref, lg_ref, r0=0):
    n = x.shape[0]
    x1 = x + mod_ref[0, 2:3, :] * y
    _tile_to_chunks(x1_ref, (0,), slice(r0, r0 + n), x1)
    h2 = _rms_mod(x1, g2_ref[...], mod_ref[0, 3:4, :], mod_ref[0, 4:5, :])
    for j in range(N_CHUNK):
        h2_ref[0, pl.ds(r0 * N_CHUNK + j, n, stride=N_CHUNK), :] = h2[:, j * LANES:(j + 1) * LANES]
    hi = h2.astype(BF16)
    lo = (h2 - hi.astype(F32)).astype(BF16)
    rw = rw_ref[...]
    a = _dot_nt(rw, hi)
    b = _dot_nt(rw[:N_EXPERTS], lo)
    lg_ref[0, :, r0:r0 + n] = a[:N_EXPERTS] + a[N_EXPERTS:] + b


def _x_spec(x_cm, tm, d):
    if x_cm:
        return pl.BlockSpec((1, N_CHUNK, tm, LANES), lambda b, i: (b, 0, i, 0))
    return pl.BlockSpec((1, tm, d), lambda b, i: (b, i, 0))


def _const_spec(shape):
    nd = len(shape)
    return pl.BlockSpec(shape, lambda b, i: (0,) * nd)


def _epilogue_specs(bsz, s, tm):
    cm_spec = pl.BlockSpec((1, N_CHUNK, tm, LANES), lambda b, i: (b, 0, i, 0))
    tt_spec = pl.BlockSpec((1, tm * N_CHUNK, LANES), lambda b, i: (b, i, 0))
    out_specs = [cm_spec, tt_spec, pl.BlockSpec((1, N_EXPERTS, tm), lambda b, i: (b, 0, i))]
    out_shape = [
        jax.ShapeDtypeStruct((bsz, N_CHUNK, s, LANES), F32),
        jax.ShapeDtypeStruct((bsz, s * N_CHUNK, LANES), F32),
        jax.ShapeDtypeStruct((bsz, N_EXPERTS, s), F32),
    ]
    return out_specs, out_shape


def _mod_spec():
    return pl.BlockSpec((1, 6, D_MODEL), lambda b, i: (b, 0, 0))


def _gmlp_kernel(x_ref, mod_ref, g1_ref, win_ref, vg_ref, sw_ref, sb_ref, wout_ref, g2_ref, rw_ref,
                 x1_ref, h2_ref, lg_ref, *, x_cm, tm):
    sub = min(tm, GMLP_SUB_TILE)
    nc = sub // CHUNK
    for r0 in range(0, tm, sub):
        x = _load_x(x_ref, x_cm, r0, sub)
        h = _rms_mod(x, g1_ref[...], mod_ref[0, 0:1, :], mod_ref[0, 1:2, :]).astype(BF16)
        z = jax.nn.gelu(_dot(h, win_ref[...]))
        u = z[:, :D_MODEL]
        v = z[:, D_MODEL:]
        mu = jnp.mean(v, axis=-1, keepdims=True)
        vc = v - mu
        v = (vc * lax.rsqrt(jnp.mean(vc * vc, axis=-1, keepdims=True) + EPS)) * vg_ref[...]
        vb = v.astype(BF16)
        mixed = []
        for g in range(A_GROUPS):
            cols = jnp.concatenate(
                [vb[c * CHUNK:(c + 1) * CHUNK, g * LANES:(g + 1) * LANES] for c in range(nc)], axis=1)
            mixed.append(_dot(sw_ref[g], cols))
        sv = jnp.concatenate(
            [jnp.concatenate([mixed[g][:, c * LANES:(c + 1) * LANES] for g in range(A_GROUPS)], axis=1)
             + sb_ref[...] for c in range(nc)], axis=0)
        y = _dot((u * sv).astype(BF16), wout_ref[...])
        _epilogue(x, y, mod_ref, g2_ref, rw_ref, x1_ref, h2_ref, lg_ref, r0)


def _gmlp_layer(x, x_cm, s, mod, g1, w_in, v_g, s_w, s_bias, w_out, g2, rw2):
    bsz = x.shape[0]
    tm = min(1024, s)
    out_specs, out_shape = _epilogue_specs(bsz, s, tm)
    return pl.pallas_call(
        functools.partial(_gmlp_kernel, x_cm=x_cm, tm=tm),
        grid=(bsz, s // tm),
        in_specs=[
            _x_spec(x_cm, tm, D_MODEL), _mod_spec(), _const_spec((1, D_MODEL)),
            _const_spec(w_in.shape), _const_spec((1, D_MODEL)), _const_spec(s_w.shape),
            _const_spec(s_bias.shape), _const_spec(w_out.shape), _const_spec((1, D_MODEL)),
            _const_spec(rw2.shape),
        ],
        out_specs=out_specs,
        out_shape=out_shape,
        compiler_params=_cparams("parallel", "arbitrary"),
        name="gmlp_mixer",
    )(x, mod, g1, w_in, v_g, s_w, s_bias, w_out, g2, rw2)


def _qkv_kernel(x_ref, mod_ref, g1_ref, w_ref, qg_ref, kg_ref, cos_ref, sin_ref, *out_refs,
                x_cm, want_q):
    x = _load_x(x_ref, x_cm)
    h = _rms_mod(x, g1_ref[...], mod_ref[0, 0:1, :], mod_ref[0, 1:2, :]).astype(BF16)
    qkv = _dot(h, w_ref[...])
    cos = cos_ref[...]
    sin = sin_ref[...]

    def norm_rope(t, g):
        t = (t * lax.rsqrt(jnp.mean(t * t, axis=-1, keepdims=True) + EPS)) * g
        return t * cos + pltpu.roll(t, HEAD_DIM // 2, 1) * sin

    qd = N_HEADS * HEAD_DIM
    kvd = N_KV_HEADS * HEAD_DIM
    if want_q:
        q_ref, k_ref, v_ref = out_refs
        for hd in range(N_HEADS):
            qh = norm_rope(qkv[:, hd * HEAD_DIM:(hd + 1) * HEAD_DIM], qg_ref[...])
            q_ref[0, :, hd * HEAD_DIM:(hd + 1) * HEAD_DIM] = (qh * (HEAD_DIM ** -0.5 * LOG2_E)).astype(BF16)
    else:
        k_ref, v_ref = out_refs
    for hd in range(N_KV_HEADS):
        kh = norm_rope(qkv[:, qd + hd * HEAD_DIM:qd + (hd + 1) * HEAD_DIM], kg_ref[...])
        k_ref[0, :, hd * HEAD_DIM:(hd + 1) * HEAD_DIM] = kh.astype(BF16)
    v_ref[0] = qkv[:, qd + kvd:].astype(BF16)


def _qkv_proj(x, x_cm, s, mod, g1, w_in, q_g, k_g, cos, sin, want_q):
    bsz = x.shape[0]
    tm = min(256, s)
    qd = N_HEADS * HEAD_DIM
    kvd = N_KV_HEADS * HEAD_DIM
    kv_spec = pl.BlockSpec((1, tm, kvd), lambda b, i: (b, i, 0))
    kv_shape = jax.ShapeDtypeStruct((bsz, s, kvd), BF16)
    out_specs = [kv_spec, kv_spec]
    out_shape = [kv_shape, kv_shape]
    if want_q:
        out_specs = [pl.BlockSpec((1, tm, qd), lambda b, i: (b, i, 0))] + out_specs
        out_shape = [jax.ShapeDtypeStruct((bsz, s, qd), BF16)] + out_shape
    return pl.pallas_call(
        functools.partial(_qkv_kernel, x_cm=x_cm, want_q=want_q),
        grid=(bsz, s // tm),
        in_specs=[
            _x_spec(x_cm, tm, D_MODEL), _mod_spec(), _const_spec((1, D_MODEL)),
            _const_spec(w_in.shape), _const_spec((1, HEAD_DIM)), _const_spec((1, HEAD_DIM)),
            pl.BlockSpec((tm, HEAD_DIM), lambda b, i: (i, 0)),
            pl.BlockSpec((tm, HEAD_DIM), lambda b, i: (i, 0)),
        ],
        out_specs=out_specs,
        out_shape=out_shape,
        compiler_params=_cparams("parallel", "arbitrary"),
        name="qkv_proj",
    )(x, mod, g1, w_in, q_g, k_g, cos, sin)


def _attn_kernel(x_ref, q_ref, k_ref, v_ref, mod_ref, wout_ref, g2_ref, rw_ref,
                 x1_ref, h2_ref, lg_ref, *, tq):
    x = _load_x(x_ref, True)
    heads = [None] * N_HEADS
    sk = k_ref.shape[1]
    ones_col = jnp.where(lax.broadcasted_iota(jnp.int32, (sk, HEAD_DIM), 1) == 0, 1.0, 0.0).astype(BF16)
    for kh in range(N_KV_HEADS):
        k = k_ref[0, :, kh * HEAD_DIM:(kh + 1) * HEAD_DIM]
        v = jnp.concatenate([v_ref[0, :, kh * HEAD_DIM:(kh + 1) * HEAD_DIM], ones_col], axis=1)
        qg = jnp.concatenate(
            [q_ref[0, :, (kh * Q_PER_KV + g) * HEAD_DIM:(kh * Q_PER_KV + g + 1) * HEAD_DIM]
             for g in range(Q_PER_KV)], axis=0)
        m = jnp.full((Q_PER_KV * tq, 1), -jnp.inf, F32)
        o = jnp.zeros((Q_PER_KV * tq, 2 * HEAD_DIM), F32)
        for kb in range(0, sk, ATTN_KEY_BLOCK):
            sc = _dot_nt(qg, k[kb:kb + ATTN_KEY_BLOCK])
            m_new = jnp.maximum(m, jnp.max(sc, axis=-1, keepdims=True))
            p = jnp.exp2(sc - m_new).astype(BF16)
            o = o * jnp.exp2(m - m_new) + _dot(p, v[kb:kb + ATTN_KEY_BLOCK])
            m = m_new
        o = o[:, :HEAD_DIM] / o[:, HEAD_DIM:HEAD_DIM + 1]
        for g in range(Q_PER_KV):
            heads[kh * Q_PER_KV + g] = o[g * tq:(g + 1) * tq].astype(BF16)
    y = _dot(jnp.concatenate(heads, axis=1), wout_ref[...])
    _epilogue(x, y, mod_ref, g2_ref, rw_ref, x1_ref, h2_ref, lg_ref)


def _attn_layer(x, s, q, k_all, v_all, mod, w_out, g2, rw2):
    bsz = x.shape[0]
    tq = 512
    sk = k_all.shape[1]
    qd = N_HEADS * HEAD_DIM
    kvd = N_KV_HEADS * HEAD_DIM
    out_specs, out_shape = _epilogue_specs(bsz, s, tq)
    return pl.pallas_call(
        functools.partial(_attn_kernel, tq=tq),
        grid=(bsz, s // tq),
        in_specs=[
            _x_spec(True, tq, D_MODEL),
            pl.BlockSpec((1, tq, qd), lambda b, i: (b, i, 0)),
            pl.BlockSpec((1, sk, kvd), lambda b, i: (b, 0, 0)),
            pl.BlockSpec((1, sk, kvd), lambda b, i: (b, 0, 0)),
            _mod_spec(), _const_spec(w_out.shape), _const_spec((1, D_MODEL)), _const_spec(rw2.shape),
        ],
        out_specs=out_specs,
        out_shape=out_shape,
        compiler_params=_cparams("parallel", "arbitrary"),
        name="gqa_attention",
    )(x, q, k_all, v_all, mod, w_out, g2, rw2)


def _pool_kernel(xp_ref, x_ref, xn_ref, mod_ref, g1_ref, win_ref, band_ref, wgrp_ref, ps_ref, g2_ref,
                 rw_ref, x1_ref, h2_ref, lg_ref, *, tm, s):
    i = pl.program_id(1)
    x = _load_x(x_ref, True)
    xe = jnp.concatenate([_load_x(xp_ref, True), x, _load_x(xn_ref, True)], axis=0)
    h = _rms_mod(xe, g1_ref[...], mod_ref[0, 0:1, :], mod_ref[0, 1:2, :]).astype(BF16)
    z = _dot(h, win_ref[...])
    te = band_ref.shape[2]
    zp = z[:POOL_HALO] * jnp.where(i > 0, 1.0, 0.0)
    zn = z[POOL_HALO + tm:] * jnp.where(i < pl.num_programs(1) - 1, 1.0, 0.0)
    zc = z[POOL_HALO:POOL_HALO + tm]
    zb = jnp.concatenate([zp, zc, zn, jnp.zeros((te - tm - 2 * POOL_HALO, D_MODEL), F32)], axis=0).astype(BF16)
    tcol = i * tm + lax.broadcasted_iota(jnp.int32, (tm, 1), 0)
    outs = []
    for gi, w in enumerate(POOL_WINDOWS):
        half = w // 2
        cnt = (jnp.minimum(tcol + half - 1, s - 1) - jnp.maximum(tcol - half, 0) + 1).astype(F32)
        lo, hi_ = gi * POOL_GROUP_DIM, (gi + 1) * POOL_GROUP_DIM
        pooled = _dot(band_ref[gi], zb[:, lo:hi_]) / cnt - zc[:, lo:hi_]
        outs.append(_dot(pooled.astype(BF16), wgrp_ref[gi]))
    y = jnp.concatenate(outs, axis=1) * ps_ref[...]
    _epilogue(x, y, mod_ref, g2_ref, rw_ref, x1_ref, h2_ref, lg_ref)


def _pool_bands(tm):
    te = tm + LANES
    t = jnp.arange(tm)[:, None]
    off = jnp.arange(te)[None, :] - POOL_HALO - t
    inside = jnp.arange(te)[None, :] < tm + 2 * POOL_HALO
    return jnp.stack([(inside & (off >= -(w // 2)) & (off <= w // 2 - 1)) for w in POOL_WINDOWS]).astype(BF16)


def _pool_layer(x, s, mod, g1, w_in, w_grp, p_scale, g2, rw2):
    bsz = x.shape[0]
    tm = 512
    nb = tm // POOL_HALO
    last = s // POOL_HALO - 1
    out_specs, out_shape = _epilogue_specs(bsz, s, tm)
    halo = (1, N_CHUNK, POOL_HALO, LANES)
    bands = _pool_bands(tm)
    return pl.pallas_call(
        functools.partial(_pool_kernel, tm=tm, s=s),
        grid=(bsz, s // tm),
        in_specs=[
            pl.BlockSpec(halo, lambda b, i: (b, 0, jnp.maximum(i * nb - 1, 0), 0)),
            _x_spec(True, tm, D_MODEL),
            pl.BlockSpec(halo, lambda b, i: (b, 0, jnp.minimum((i + 1) * nb, last), 0)),
            _mod_spec(), _const_spec((1, D_MODEL)), _const_spec(w_in.shape), _const_spec(bands.shape),
            _const_spec(w_grp.shape), _const_spec((1, D_MODEL)), _const_spec((1, D_MODEL)),
            _const_spec(rw2.shape),
        ],
        out_specs=out_specs,
        out_shape=out_shape,
        compiler_params=_cparams("parallel", "arbitrary"),
        name="pool_mixer",
    )(x, x, x, mod, g1, w_in, bands, w_grp, p_scale, g2, rw2)


SORT_PHASE_BITS = 4


def _comes_first(a, ia, b, ib):
    return (a > b) | ((a == b) & (ia < ib))


def _sort_tokens(key_ref, tok_ref, sub, n):
    nv = key_ref.shape[0]
    vbits = nv.bit_length() - 1

    def exchange(ka, ta, kb, tb, inv):
        keep = _comes_first(ka, ta, kb, tb)
        if inv is not None:
            keep = jnp.logical_xor(keep, inv)
        return jnp.where(keep, ka, kb), jnp.where(keep, ta, tb), jnp.where(keep, kb, ka), jnp.where(keep, tb, ta)

    k = 2
    while k <= n:
        dists = [k >> d for d in range(1, k.bit_length())]
        sub_inv = None if k >= n else ((sub & (k >> vbits)) != 0 if k >= nv else False)

        for j in [d for d in dists if d >= nv]:
            js = j >> vbits
            lower = (sub & js) == 0
            later = jnp.logical_not(lower) if sub_inv is None else jnp.logical_xor(lower, jnp.logical_not(sub_inv))

            def rotate_stage(g, carry, js=js, lower=lower, later=later):
                for u in range(SUBLANES):
                    v = g * SUBLANES + u
                    x, t = key_ref[v], tok_ref[v]
                    px = jnp.where(lower, pltpu.roll(x, SUBLANES - js, 0), pltpu.roll(x, js, 0))
                    pt = jnp.where(lower, pltpu.roll(t, SUBLANES - js, 0), pltpu.roll(t, js, 0))
                    keep = jnp.logical_xor(_comes_first(x, t, px, pt), later)
                    key_ref[v] = jnp.where(keep, x, px)
                    tok_ref[v] = jnp.where(keep, t, pt)
                return carry

            lax.fori_loop(0, nv // SUBLANES, rotate_stage, 0)

        vreg_dists = [d for d in dists if d < nv]
        for p0 in range(0, len(vreg_dists), SORT_PHASE_BITS):
            phase = vreg_dists[p0:p0 + SORT_PHASE_BITS]
            lo = phase[-1].bit_length() - 1
            nb = len(phase)
            members = [m << lo for m in range(1 << nb)]

            def group_phase(q, carry, phase=phase, lo=lo, nb=nb, members=members, k=k, sub_inv=sub_inv):
                base = ((q >> lo) << (lo + nb)) | (q & ((1 << lo) - 1))
                if sub_inv is False:
                    bit = (base >> (k.bit_length() - 1)) & 1
                    inv = lax.broadcast(bit, (SUBLANES, LANES)) != 0
                else:
                    inv = sub_inv
                ks = {m: key_ref[base + m] for m in members}
                ts = {m: tok_ref[base + m] for m in members}
                for j in phase:
                    for m in members:
                        if not m & j:
                            ks[m], ts[m], ks[m | j], ts[m | j] = exchange(ks[m], ts[m], ks[m | j], ts[m | j], inv)
                for m in members:
                    key_ref[base + m] = ks[m]
                    tok_ref[base + m] = ts[m]
                return carry

            lax.fori_loop(0, nv >> nb, group_phase, 0)
        k *= 2


def _route_kernel(lg_ref, idx_ref, gate_ref, key_ref, tok_ref, *, s, cap):
    lg = lg_ref[...]
    ex = jnp.exp(lg - jnp.max(lg, axis=1, keepdims=True))
    aff = ex / jnp.sum(ex, axis=1, keepdims=True)
    nv = s // SUBLANES
    key_ref[...] = jnp.transpose(aff.reshape(LANES, s)).reshape(nv, SUBLANES, LANES)
    tok_ref[...] = (lax.broadcasted_iota(jnp.int32, (nv, SUBLANES, LANES), 0) * SUBLANES
                    + lax.broadcasted_iota(jnp.int32, (nv, SUBLANES, LANES), 1))
    sub = lax.broadcasted_iota(jnp.int32, (SUBLANES, LANES), 0)
    _sort_tokens(key_ref, tok_ref, sub, s)
    top_k = key_ref[:, 0, :]
    top_i = tok_ref[:, 0, :].astype(F32)
    pad = LANES - cap % LANES if cap % LANES else 0
    if pad:
        top_k = jnp.concatenate([top_k, jnp.zeros((pad, LANES), F32)], axis=0)
        top_i = jnp.concatenate([top_i, jnp.zeros((pad, LANES), F32)], axis=0)
    gate_ref[...] = jnp.transpose(top_k)[:, :cap]
    idx_ref[...] = jnp.transpose(top_i)[:, :cap].astype(jnp.int32)


def _route(logits_t, s):
    bsz = logits_t.shape[0]
    cap = CAPACITY_FACTOR * s // N_EXPERTS
    n_rows = bsz * N_EXPERTS
    assert n_rows == LANES and cap == s // SUBLANES
    return pl.pallas_call(
        functools.partial(_route_kernel, s=s, cap=cap),
        out_shape=[
            jax.ShapeDtypeStruct((n_rows, cap), jnp.int32),
            jax.ShapeDtypeStruct((n_rows, cap), F32),
        ],
        scratch_shapes=[pltpu.VMEM((s // SUBLANES, SUBLANES, LANES), F32),
                        pltpu.VMEM((s // SUBLANES, SUBLANES, LANES), jnp.int32)],
        compiler_params=pltpu.CompilerParams(vmem_limit_bytes=VMEM_LIMIT),
        name="expert_choice_route",
    )(logits_t)


MOE_TILES = 2
ROW_SLOTS = 3


def _moe_kernel(idx_cur, idx_nxt, idx_nxt2, h2_hbm, wg_hbm, wu_hbm, wd_hbm, o_ref,
                xin, stage_g, stage_u, stage_d, wg, wu, wd, sem_x, sem_w,
                *, layer, n_b, out_tiles, rows):
    e = pl.program_id(0)
    b = pl.program_id(1)
    step = e * n_b + b
    n_steps = N_EXPERTS * n_b
    n_rows = out_tiles * rows
    slot = step % ROW_SLOTS
    ahead = (step + ROW_SLOTS - 1) % ROW_SLOTS

    def row_copy(idx_ref, r, dst_slot):
        src_row = pl.multiple_of(idx_ref[0, 0, r] * N_CHUNK, N_CHUNK)
        return pltpu.make_async_copy(h2_hbm.at[pl.ds(src_row, N_CHUNK), :],
                                     xin.at[dst_slot, pl.ds(r * N_CHUNK, N_CHUNK), :],
                                     sem_x.at[dst_slot])

    def wait_rows(s_):
        pltpu.make_async_copy(h2_hbm.at[pl.ds(0, n_rows * N_CHUNK), :], xin.at[s_], sem_x.at[s_]).wait()

    def weight_copies(expert, ws):
        return [pltpu.make_async_copy(w.at[layer, expert], st.at[ws], sem_w.at[ws])
                for w, st in ((wg_hbm, stage_g), (wu_hbm, stage_u), (wd_hbm, stage_d))]

    @pl.when(step == 0)
    def _():
        for cp in weight_copies(0, 0):
            cp.start()

        def issue(r, carry):
            row_copy(idx_cur, r, 0).start()
            row_copy(idx_nxt, r, 1).start()
            return carry

        lax.fori_loop(0, n_rows, issue, 0)

    @pl.when(b == 0)
    def _():
        ws = e % 2
        for cp in weight_copies(e, ws):
            cp.wait()

        @pl.when(e + 1 < N_EXPERTS)
        def _():
            for cp in weight_copies(e + 1, 1 - ws):
                cp.start()

        rb = 128

        def cast(i, carry):
            rws = pl.ds(pl.multiple_of(i * rb, rb), rb)
            wg[rws, :] = stage_g[ws, rws, :].astype(BF16)
            wu[rws, :] = stage_u[ws, rws, :].astype(BF16)
            wd[rws, :] = stage_d[ws, rws, :].astype(BF16)
            return carry

        lax.fori_loop(0, D_MODEL // rb, cast, 0)

    wait_rows(slot)
    xs = xin.at[slot]
    x = jnp.concatenate([xs[pl.ds(j, n_rows, stride=N_CHUNK), :] for j in range(N_CHUNK)], axis=1).astype(BF16)
    hid = (jax.nn.silu(_dot(x, wg[...])) * _dot(x, wu[...])).astype(BF16)
    y = _dot(hid, wd[...])
    s2 = rows + ROW_PAD
    for n in range(out_tiles):
        for j in range(N_CHUNK):
            o_ref[0, 0, n, pl.ds(j * s2, rows), :] = y[n * rows:(n + 1) * rows, j * LANES:(j + 1) * LANES]
            o_ref[0, 0, n, pl.ds(j * s2 + rows, ROW_PAD), :] = jnp.zeros((ROW_PAD, LANES), F32)

    for r in range(n_rows):
        row_copy(idx_nxt2, r, ahead).start(priority=r % 2)

    @pl.when(step == n_steps - 1)
    def _():
        wait_rows(ahead)
        wait_rows((step + 1) % ROW_SLOTS)


def _moe_ffn(rows_glob, h2_tt, w_gate, w_up, w_down, layer, out_tiles, rows):
    n_b = rows_glob.shape[0] // N_EXPERTS
    n_rows = out_tiles * rows
    n_steps = N_EXPERTS * n_b
    s2 = rows + ROW_PAD
    wshape = w_gate.shape[2:]
    smem = functools.partial(pl.BlockSpec, memory_space=pltpu.SMEM)
    hbm = pl.BlockSpec(memory_space=pl.ANY)
    return pl.pallas_call(
        functools.partial(_moe_kernel, layer=layer, n_b=n_b, out_tiles=out_tiles, rows=rows),
        grid=(N_EXPERTS, n_b),
        in_specs=[
            smem((1, 1, n_rows), lambda e, b: (e * n_b + b, 0, 0)),
            smem((1, 1, n_rows), lambda e, b: (jnp.minimum(e * n_b + b + 1, n_steps - 1), 0, 0)),
            smem((1, 1, n_rows), lambda e, b: (jnp.minimum(e * n_b + b + 2, n_steps - 1), 0, 0)),
            hbm, hbm, hbm, hbm,
        ],
        out_specs=pl.BlockSpec((1, 1, out_tiles, N_CHUNK * s2, LANES), lambda e, b: (b, e, 0, 0, 0)),
        out_shape=jax.ShapeDtypeStruct((n_b, N_EXPERTS, out_tiles, N_CHUNK * s2, LANES), F32),
        scratch_shapes=[
            pltpu.VMEM((ROW_SLOTS, n_rows * N_CHUNK, LANES), F32),
            pltpu.VMEM((2,) + wshape, F32), pltpu.VMEM((2,) + wshape, F32), pltpu.VMEM((2,) + wshape, F32),
            pltpu.VMEM(wshape, BF16), pltpu.VMEM(wshape, BF16), pltpu.VMEM(wshape, BF16),
            pltpu.SemaphoreType.DMA((ROW_SLOTS,)), pltpu.SemaphoreType.DMA((2,)),
        ],
        compiler_params=_cparams("arbitrary", "arbitrary"),
        name="expert_ffn",
    )(rows_glob, rows_glob, rows_glob, h2_tt, w_gate, w_up, w_down)


def _global_rows(idx, bsz, s, tiles):
    cap = idx.shape[-1]
    glob = idx.reshape(bsz, N_EXPERTS, cap) + (jnp.arange(bsz, dtype=jnp.int32) * s)[:, None, None]
    return _expert_major(glob, bsz, tiles)


def _expert_major(per_row, bsz, tiles):
    cap = per_row.shape[-1]
    return jnp.swapaxes(per_row.reshape(bsz, N_EXPERTS, cap), 0, 1).reshape(
        N_EXPERTS * (bsz // tiles), 1, tiles * cap)


COMBINE_UNROLL = 8
COMBINE_EXPERTS = 4


def _combine_kernel(idx_ref, gate_ref, x_ref, y_ref, mod_ref, o_ref, acc, *, s, sp, cap, s2, n_seg, to_rows):
    eg = pl.program_id(1)
    base = pl.program_id(0) * cap if n_seg > 1 else 0

    @pl.when(eg == 0)
    def _():
        for j in range(N_CHUNK):
            acc[pl.ds(j * sp, s), :] = x_ref[0, j]
            acc[pl.ds(j * sp + s, ROW_PAD), :] = jnp.zeros((ROW_PAD, LANES), F32)

    g2 = mod_ref[0, 5]

    for k in range(COMBINE_EXPERTS):
        y = y_ref.at[0, k, 0]

        def body(i, carry):
            new = []
            for u in range(COMBINE_UNROLL):
                r = i * COMBINE_UNROLL + u
                t = idx_ref[0, 0, k * cap + r]
                yrow = y[pl.ds(base + r, N_CHUNK, stride=s2), :]
                cur = acc[pl.ds(t, N_CHUNK, stride=sp), :]
                new.append((t, cur + (gate_ref[0, 0, k * cap + r] * g2) * yrow))
            for t, val in new:
                acc[pl.ds(t, N_CHUNK, stride=sp), :] = val
            return carry

        lax.fori_loop(0, cap // COMBINE_UNROLL, body, 0)

    @pl.when(eg == N_EXPERTS // COMBINE_EXPERTS - 1)
    def _():
        for j in range(N_CHUNK):
            if to_rows:
                o_ref[0, :, j * LANES:(j + 1) * LANES] = acc[pl.ds(j * sp, s), :]
            else:
                o_ref[0, j] = acc[pl.ds(j * sp, s), :]


def _combine(idx, gate, x1cm, ycm, mod, s, to_rows):
    bsz = x1cm.shape[0]
    sp = s + ROW_PAD
    cap = idx.shape[-1]
    n_b, _, tiles, rows8, _ = ycm.shape
    s2 = rows8 // N_CHUNK
    n_seg = bsz // (n_b * tiles)
    eg = COMBINE_EXPERTS
    y_spec = pl.BlockSpec((1, eg, 1, rows8, LANES), lambda b, g: (b // tiles // n_seg, g, b % tiles, 0, 0))
    smem = functools.partial(pl.BlockSpec, memory_space=pltpu.SMEM)
    cm_spec = pl.BlockSpec((1, N_CHUNK, s, LANES), lambda b, g: (b, 0, 0, 0))
    if to_rows:
        out_spec = pl.BlockSpec((1, s, D_MODEL), lambda b, g: (b, 0, 0))
        out_shape = jax.ShapeDtypeStruct((bsz, s, D_MODEL), F32)
    else:
        out_spec = cm_spec
        out_shape = jax.ShapeDtypeStruct((bsz, N_CHUNK, s, LANES), F32)
    n_grp = N_EXPERTS // eg
    return pl.pallas_call(
        functools.partial(_combine_kernel, s=s, sp=sp, cap=cap, s2=s2, n_seg=n_seg, to_rows=to_rows),
        grid=(bsz, n_grp),
        in_specs=[
            smem((1, 1, eg * cap), lambda b, g: (b * n_grp + g, 0, 0)),
            smem((1, 1, eg * cap), lambda b, g: (b * n_grp + g, 0, 0)),
            cm_spec,
            y_spec,
            pl.BlockSpec((1, 6, N_CHUNK, LANES), lambda b, g: (b, 0, 0, 0)),
        ],
        out_specs=out_spec,
        out_shape=out_shape,
        scratch_shapes=[pltpu.VMEM((N_CHUNK * sp, LANES), F32)],
        compiler_params=_cparams("parallel", "arbitrary"),
        name="combine_rows" if to_rows else "combine_cm",
    )(idx.reshape(bsz * n_grp, 1, eg * cap), gate.reshape(bsz * n_grp, 1, eg * cap),
      x1cm, ycm, mod.reshape(bsz, 6, N_CHUNK, LANES))


def _rope_tables(n):
    rows = n // GRID_W
    row = jnp.repeat(jnp.arange(rows, dtype=F32), GRID_W)
    col = jnp.tile(jnp.arange(GRID_W, dtype=F32), rows)
    n_freq = HEAD_DIM // 4
    inv = jnp.power(ROPE_THETA, -jnp.arange(n_freq, dtype=F32) / n_freq)
    ang = jnp.concatenate([row[:, None] * inv, col[:, None] * inv], axis=-1)
    cos, sin = jnp.cos(ang), jnp.sin(ang)
    return jnp.concatenate([cos, cos], axis=-1), jnp.concatenate([-sin, sin], axis=-1)


def _router_rows(router_w):
    wt = router_w.T
    hi = wt.astype(BF16)
    lo = (wt - hi.astype(F32)).astype(BF16)
    return jnp.concatenate([hi, lo], axis=0)


def kernel(x, c, ctx, c_ctx, mod_w, mod_b, norm1_g, norm2_g, router_w, exp_w_gate, exp_w_up, exp_w_down,
           g_w_in, g_v_norm_g, g_spatial_w, g_spatial_b, g_w_out, att_w_in, att_q_norm_g, att_k_norm_g,
           att_w_out, p_w_in, p_w_group, p_scale):
    bsz, s, d = x.shape
    s_ctx = ctx.shape[1]
    depth = mod_w.shape[0]
    assert d == D_MODEL and bsz <= 8 and bsz % MOE_TILES == 0 and s % 512 == 0 and s_ctx % 128 == 0

    cc = jnp.zeros((16, d), F32).at[:bsz].set(c).at[bsz].set(c_ctx)
    mods = _modulation(cc, mod_w, mod_b).reshape(depth, 16, 6, d)

    x_cur, x_cm = x, False
    c_cur, c_cm = ctx, False
    for i in range(depth):
        kind, j = i % 3, i // 3
        update_ctx = any(l % 3 == 1 for l in range(i + 1, depth))
        last = i == depth - 1
        mod_lat = mods[i, :bsz]
        mod_ctx = jnp.broadcast_to(mods[i, bsz][None], (bsz, 6, d))
        g1 = norm1_g[i][None]
        g2 = norm2_g[i][None]
        rw2 = _router_rows(router_w[i])

        ctx_out = None
        if kind == 0:
            s_bias = jnp.repeat(g_spatial_b[j].T, LANES, axis=1)
            gm = functools.partial(
                _gmlp_layer, g1=g1, w_in=g_w_in[j].astype(BF16), v_g=g_v_norm_g[j][None],
                s_w=g_spatial_w[j].astype(BF16), s_bias=s_bias, w_out=g_w_out[j].astype(BF16),
                g2=g2, rw2=rw2)
            x1, h2, lg = gm(x_cur, x_cm, s, mod_lat)
            if update_ctx:
                ctx_out = gm(c_cur, c_cm, s_ctx, mod_ctx)
        elif kind == 1:
            assert not update_ctx
            cos, sin = _rope_tables(s)
            w_in = att_w_in[j].astype(BF16)
            qg, kg = att_q_norm_g[j][None], att_k_norm_g[j][None]
            q, k, v = _qkv_proj(x_cur, x_cm, s, mod_lat, g1, w_in, qg, kg, cos, sin, True)
            kc, vc = _qkv_proj(c_cur, c_cm, s_ctx, mod_ctx, g1, w_in, qg, kg,
                               jnp.ones((s_ctx, HEAD_DIM), F32), jnp.zeros((s_ctx, HEAD_DIM), F32), False)
            k_all = jnp.concatenate([kc, k], axis=1)
            v_all = jnp.concatenate([vc, v], axis=1)
            assert x_cm
            x1, h2, lg = _attn_layer(x_cur, s, q, k_all, v_all, mod_lat, att_w_out[j].astype(BF16), g2, rw2)
        else:
            assert not update_ctx and x_cm
            x1, h2, lg = _pool_layer(x_cur, s, mod_lat, g1, p_w_in[j].astype(BF16),
                                     p_w_group[j].astype(BF16), p_scale[j][None], g2, rw2)

        experts = functools.partial(_moe_ffn, w_gate=exp_w_gate, w_up=exp_w_up, w_down=exp_w_down, layer=i)
        idx, gate = _route(lg, s)
        ys = experts(_global_rows(idx, bsz, s, MOE_TILES), h2.reshape(bsz * s * N_CHUNK, LANES),
                     out_tiles=MOE_TILES, rows=idx.shape[-1])
        x_cur = _combine(idx, gate, x1, ys, mod_lat, s, last)
        x_cm = True
        if ctx_out is not None:
            c1, hc2, lgc = ctx_out
            idx_c, gate_c = _route(lgc, s_ctx)
            ys_c = experts(_global_rows(idx_c, bsz, s_ctx, bsz), hc2.reshape(bsz * s_ctx * N_CHUNK, LANES),
                           out_tiles=1, rows=bsz * idx_c.shape[-1])
            c_cur = _combine(idx_c, gate_c, c1, ys_c, mod_ctx, s_ctx, False)
            c_cm = True
    return x_cur
```

```python
import functools

import jax
import jax.numpy as jnp
from jax import lax
from jax.experimental import pallas as pl
from jax.experimental.pallas import tpu as pltpu

F32 = jnp.float32
BF16 = jnp.bfloat16

D_MODEL = 1024
LANES = 128
SUBLANES = 8
N_CHUNK = D_MODEL // LANES
ROW_PAD = 8
EPS = 1e-6
GRID_W = 64
CHUNK = 128
A_GROUPS = 8
HEAD_DIM = 128
N_HEADS = 8
N_KV_HEADS = 2
Q_PER_KV = N_HEADS // N_KV_HEADS
ROPE_THETA = 10000.0
LOG2_E = 1.4426950408889634
GMLP_TILE = 1024
GMLP_SUB_TILE = 512
QKV_TILE = 256
ATTN_Q_TILE = 512
ATTN_HEAD_GROUP = Q_PER_KV
ATTN_KEY_BLOCK = 768
POOL_TILE = 512
POOL_WINDOWS = (2, 4, 8, 16)
POOL_GROUP_DIM = D_MODEL // len(POOL_WINDOWS)
POOL_HALO = 8
N_EXPERTS = 16
CAPACITY_FACTOR = 2
VMEM_LIMIT = 56 * 1024 * 1024


def _cparams(*sem):
    return pltpu.CompilerParams(dimension_semantics=sem, vmem_limit_bytes=VMEM_LIMIT)


def _dot(a, b):
    return jnp.dot(a, b, preferred_element_type=F32)


def _dot_nt(a, b):
    return lax.dot_general(a, b, (((1,), (1,)), ((), ())), preferred_element_type=F32)


def _chunks_to_tile(ref, lead, rows):
    return jnp.concatenate([ref[lead + (j, rows)] for j in range(N_CHUNK)], axis=-1)


def _tile_to_chunks(ref, lead, rows, val):
    for j in range(N_CHUNK):
        ref[lead + (j, rows)] = val[:, j * LANES:(j + 1) * LANES]


def _rms_mod(x, g, shift, scale):
    y = x * lax.rsqrt(jnp.mean(x * x, axis=-1, keepdims=True) + EPS)
    return (y * g) * (1.0 + scale) + shift


def _mod_kernel(c_ref, w_ref, b_ref, o_ref):
    a = jax.nn.silu(c_ref[...]).astype(BF16)
    o_ref[0] = _dot(a, w_ref[0].astype(BF16)) + b_ref[0]


def _modulation(cc, mod_w, mod_b):
    depth, d, n = mod_w.shape
    tn = 1536
    return pl.pallas_call(
        _mod_kernel,
        grid=(depth, n // tn),
        in_specs=[
            pl.BlockSpec((16, d), lambda l, j: (0, 0)),
            pl.BlockSpec((1, d, tn), lambda l, j: (l, 0, j)),
            pl.BlockSpec((1, 1, tn), lambda l, j: (l, 0, j)),
        ],
        out_specs=pl.BlockSpec((1, 16, tn), lambda l, j: (l, 0, j)),
        out_shape=jax.ShapeDtypeStruct((depth, 16, n), F32),
        compiler_params=_cparams("parallel", "parallel"),
        name="adaln_modulation",
    )(cc, mod_w, mod_b.reshape(depth, 1, n))


def _load_x(x_ref, x_cm, r0=0, n=None):
    rows = slice(None) if n is None else slice(r0, r0 + n)
    if x_cm:
        return _chunks_to_tile(x_ref, (0,), rows)
    return x_ref[0, rows]


def _epilogue(x, y, mod_ref, g2_ref, rw_ref, x1_ref, h2_ref, lg_ref, r0=0):
    n = x.shape[0]
    x1 = x + mod_ref[0, 2:3, :] * y
    _tile_to_chunks(x1_ref, (0,), slice(r0, r0 + n), x1)
    h2 = _rms_mod(x1, g2_ref[...], mod_ref[0, 3:4, :], mod_ref[0, 4:5, :])
    for j in range(N_CHUNK):
        h2_ref[0, pl.ds(r0 * N_CHUNK + j, n, stride=N_CHUNK), :] = h2[:, j * LANES:(j + 1) * LANES]
    hi = h2.astype(BF16)
    lo = (h2 - hi.astype(F32)).astype(BF16)
    rw = rw_ref[...]
    a = _dot_nt(rw, hi)
    b = _dot_nt(rw[:N_EXPERTS], lo)
    lg_ref[0, :, r0:r0 + n] = a[:N_EXPERTS] + a[N_EXPERTS:] + b


def _x_spec(x_cm, tm, d):
    if x_cm:
        return pl.BlockSpec((1, N_CHUNK, tm, LANES), lambda b, i: (b, 0, i, 0))
    return pl.BlockSpec((1, tm, d), lambda b, i: (b, i, 0))


def _const_spec(shape):
    nd = len(shape)
    return pl.BlockSpec(shape, lambda b, i: (0,) * nd)


def _epilogue_specs(bsz, s, tm):
    cm_spec = pl.BlockSpec((1, N_CHUNK, tm, LANES), lambda b, i: (b, 0, i, 0))
    tt_spec = pl.BlockSpec((1, tm * N_CHUNK, LANES), lambda b, i: (b, i, 0))
    out_specs = [cm_spec, tt_spec, pl.BlockSpec((1, N_EXPERTS, tm), lambda b, i: (b, 0, i))]
    out_shape = [
        jax.ShapeDtypeStruct((bsz, N_CHUNK, s, LANES), F32),
        jax.ShapeDtypeStruct((bsz, s * N_CHUNK, LANES), F32),
        jax.ShapeDtypeStruct((bsz, N_EXPERTS, s), F32),
    ]
    return out_specs, out_shape


def _mod_spec():
    return pl.BlockSpec((1, 6, D_MODEL), lambda b, i: (b, 0, 0))


def _gmlp_kernel(x_ref, mod_ref, g1_ref, win_ref, vg_ref, sw_ref, sb_ref, wout_ref, g2_ref, rw_ref,
                 x1_ref, h2_ref, lg_ref, *, x_cm, tm):
    sub = min(tm, GMLP_SUB_TILE)
    nc = sub // CHUNK
    for r0 in range(0, tm, sub):
        x = _load_x(x_ref, x_cm, r0, sub)
        h = _rms_mod(x, g1_ref[...], mod_ref[0, 0:1, :], mod_ref[0, 1:2, :]).astype(BF16)
        z = jax.nn.gelu(_dot(h, win_ref[...]))
        u = z[:, :D_MODEL]
        v = z[:, D_MODEL:]
        mu = jnp.mean(v, axis=-1, keepdims=True)
        vc = v - mu
        v = (vc * lax.rsqrt(jnp.mean(vc * vc, axis=-1, keepdims=True) + EPS)) * vg_ref[...]
        vb = v.astype(BF16)
        mixed = []
        for g in range(A_GROUPS):
            cols = jnp.concatenate(
                [vb[c * CHUNK:(c + 1) * CHUNK, g * LANES:(g + 1) * LANES] for c in range(nc)], axis=1)
            mixed.append(_dot(sw_ref[g], cols))
        sv = jnp.concatenate(
            [jnp.concatenate([mixed[g][:, c * LANES:(c + 1) * LANES] for g in range(A_GROUPS)], axis=1)
             + sb_ref[...] for c in range(nc)], axis=0)
        y = _dot((u * sv).astype(BF16), wout_ref[...])
        _epilogue(x, y, mod_ref, g2_ref, rw_ref, x1_ref, h2_ref, lg_ref, r0)


def _gmlp_layer(x, x_cm, s, mod, g1, w_in, v_g, s_w, s_bias, w_out, g2, rw2):
    bsz = x.shape[0]
    tm = min(GMLP_TILE, s)
    out_specs, out_shape = _epilogue_specs(bsz, s, tm)
    return pl.pallas_call(
        functools.partial(_gmlp_kernel, x_cm=x_cm, tm=tm),
        grid=(bsz, s // tm),
        in_specs=[
            _x_spec(x_cm, tm, D_MODEL), _mod_spec(), _const_spec((1, D_MODEL)),
            _const_spec(w_in.shape), _const_spec((1, D_MODEL)), _const_spec(s_w.shape),
            _const_spec(s_bias.shape), _const_spec(w_out.shape), _const_spec((1, D_MODEL)),
            _const_spec(rw2.shape),
        ],
        out_specs=out_specs,
        out_shape=out_shape,
        compiler_params=_cparams("parallel", "arbitrary"),
        name="gmlp_mixer",
    )(x, mod, g1, w_in, v_g, s_w, s_bias, w_out, g2, rw2)


def _qkv_kernel(x_ref, mod_ref, g1_ref, w_ref, qg_ref, kg_ref, cos_ref, sin_ref, *out_refs,
                x_cm, want_q):
    x = _load_x(x_ref, x_cm)
    h = _rms_mod(x, g1_ref[...], mod_ref[0, 0:1, :], mod_ref[0, 1:2, :]).astype(BF16)
    qkv = _dot(h, w_ref[...])
    cos = cos_ref[...]
    sin = sin_ref[...]

    def norm_rope(t, g):
        t = (t * lax.rsqrt(jnp.mean(t * t, axis=-1, keepdims=True) + EPS)) * g
        return t * cos + pltpu.roll(t, HEAD_DIM // 2, 1) * sin

    qd = N_HEADS * HEAD_DIM
    kvd = N_KV_HEADS * HEAD_DIM
    if want_q:
        q_ref, k_ref, v_ref = out_refs
        for hd in range(N_HEADS):
            qh = norm_rope(qkv[:, hd * HEAD_DIM:(hd + 1) * HEAD_DIM], qg_ref[...])
            q_ref[0, :, hd * HEAD_DIM:(hd + 1) * HEAD_DIM] = (qh * (HEAD_DIM ** -0.5 * LOG2_E)).astype(BF16)
    else:
        k_ref, v_ref = out_refs
    for hd in range(N_KV_HEADS):
        kh = norm_rope(qkv[:, qd + hd * HEAD_DIM:qd + (hd + 1) * HEAD_DIM], kg_ref[...])
        k_ref[0, :, hd * HEAD_DIM:(hd + 1) * HEAD_DIM] = kh.astype(BF16)
    v_ref[0] = qkv[:, qd + kvd:].astype(BF16)


def _qkv_proj(x, x_cm, s, mod, g1, w_in, q_g, k_g, cos, sin, want_q):
    bsz = x.shape[0]
    tm = min(QKV_TILE, s)
    qd = N_HEADS * HEAD_DIM
    kvd = N_KV_HEADS * HEAD_DIM
    kv_spec = pl.BlockSpec((1, tm, kvd), lambda b, i: (b, i, 0))
    kv_shape = jax.ShapeDtypeStruct((bsz, s, kvd), BF16)
    out_specs = [kv_spec, kv_spec]
    out_shape = [kv_shape, kv_shape]
    if want_q:
        out_specs = [pl.BlockSpec((1, tm, qd), lambda b, i: (b, i, 0))] + out_specs
        out_shape = [jax.ShapeDtypeStruct((bsz, s, qd), BF16)] + out_shape
    return pl.pallas_call(
        functools.partial(_qkv_kernel, x_cm=x_cm, want_q=want_q),
        grid=(bsz, s // tm),
        in_specs=[
            _x_spec(x_cm, tm, D_MODEL), _mod_spec(), _const_spec((1, D_MODEL)),
            _const_spec(w_in.shape), _const_spec((1, HEAD_DIM)), _const_spec((1, HEAD_DIM)),
            pl.BlockSpec((tm, HEAD_DIM), lambda b, i: (i, 0)),
            pl.BlockSpec((tm, HEAD_DIM), lambda b, i: (i, 0)),
        ],
        out_specs=out_specs,
        out_shape=out_shape,
        compiler_params=_cparams("parallel", "arbitrary"),
        name="qkv_proj",
    )(x, mod, g1, w_in, q_g, k_g, cos, sin)


def _attn_kernel(x_ref, q_ref, k_ref, v_ref, mod_ref, wout_ref, g2_ref, rw_ref,
                 x1_ref, h2_ref, lg_ref, *, tq):
    x = _load_x(x_ref, True)
    heads = [None] * N_HEADS
    sk = k_ref.shape[1]
    ones_col = jnp.where(lax.broadcasted_iota(jnp.int32, (sk, HEAD_DIM), 1) == 0, 1.0, 0.0).astype(BF16)
    for h0 in range(0, N_HEADS, ATTN_HEAD_GROUP):
        kh = h0 // Q_PER_KV
        k = k_ref[0, :, kh * HEAD_DIM:(kh + 1) * HEAD_DIM]
        v = jnp.concatenate([v_ref[0, :, kh * HEAD_DIM:(kh + 1) * HEAD_DIM], ones_col], axis=1)
        qg = jnp.concatenate(
            [q_ref[0, :, (h0 + g) * HEAD_DIM:(h0 + g + 1) * HEAD_DIM] for g in range(ATTN_HEAD_GROUP)], axis=0)
        m = jnp.full((ATTN_HEAD_GROUP * tq, 1), -jnp.inf, F32)
        o = jnp.zeros((ATTN_HEAD_GROUP * tq, 2 * HEAD_DIM), F32)
        for kb in range(0, sk, ATTN_KEY_BLOCK):
            sc = _dot_nt(qg, k[kb:kb + ATTN_KEY_BLOCK])
            m_new = jnp.maximum(m, jnp.max(sc, axis=-1, keepdims=True))
            p = jnp.exp2(sc - m_new).astype(BF16)
            o = o * jnp.exp2(m - m_new) + _dot(p, v[kb:kb + ATTN_KEY_BLOCK])
            m = m_new
        o = o[:, :HEAD_DIM] / o[:, HEAD_DIM:HEAD_DIM + 1]
        for g in range(ATTN_HEAD_GROUP):
            heads[h0 + g] = o[g * tq:(g + 1) * tq].astype(BF16)
    y = _dot(jnp.concatenate(heads, axis=1), wout_ref[...])
    _epilogue(x, y, mod_ref, g2_ref, rw_ref, x1_ref, h2_ref, lg_ref)


def _attn_layer(x, s, q, k_all, v_all, mod, w_out, g2, rw2):
    bsz = x.shape[0]
    tq = ATTN_Q_TILE
    sk = k_all.shape[1]
    qd = N_HEADS * HEAD_DIM
    kvd = N_KV_HEADS * HEAD_DIM
    out_specs, out_shape = _epilogue_specs(bsz, s, tq)
    return pl.pallas_call(
        functools.partial(_attn_kernel, tq=tq),
        grid=(bsz, s // tq),
        in_specs=[
            _x_spec(True, tq, D_MODEL),
            pl.BlockSpec((1, tq, qd), lambda b, i: (b, i, 0)),
            pl.BlockSpec((1, sk, kvd), lambda b, i: (b, 0, 0)),
            pl.BlockSpec((1, sk, kvd), lambda b, i: (b, 0, 0)),
            _mod_spec(), _const_spec(w_out.shape), _const_spec((1, D_MODEL)), _const_spec(rw2.shape),
        ],
        out_specs=out_specs,
        out_shape=out_shape,
        compiler_params=_cparams("parallel", "arbitrary"),
        name="gqa_attention",
    )(x, q, k_all, v_all, mod, w_out, g2, rw2)


def _pool_kernel(xp_ref, x_ref, xn_ref, mod_ref, g1_ref, win_ref, band_ref, wgrp_ref, ps_ref, g2_ref,
                 rw_ref, x1_ref, h2_ref, lg_ref, *, tm, s):
    i = pl.program_id(1)
    x = _load_x(x_ref, True)
    xe = jnp.concatenate([_load_x(xp_ref, True), x, _load_x(xn_ref, True)], axis=0)
    h = _rms_mod(xe, g1_ref[...], mod_ref[0, 0:1, :], mod_ref[0, 1:2, :]).astype(BF16)
    z = _dot(h, win_ref[...])
    te = band_ref.shape[2]
    zp = z[:POOL_HALO] * jnp.where(i > 0, 1.0, 0.0)
    zn = z[POOL_HALO + tm:] * jnp.where(i < pl.num_programs(1) - 1, 1.0, 0.0)
    zc = z[POOL_HALO:POOL_HALO + tm]
    zb = jnp.concatenate([zp, zc, zn, jnp.zeros((te - tm - 2 * POOL_HALO, D_MODEL), F32)], axis=0).astype(BF16)
    tcol = i * tm + lax.broadcasted_iota(jnp.int32, (tm, 1), 0)
    outs = []
    for gi, w in enumerate(POOL_WINDOWS):
        half = w // 2
        cnt = (jnp.minimum(tcol + half - 1, s - 1) - jnp.maximum(tcol - half, 0) + 1).astype(F32)
        lo, hi_ = gi * POOL_GROUP_DIM, (gi + 1) * POOL_GROUP_DIM
        pooled = _dot(band_ref[gi], zb[:, lo:hi_]) / cnt - zc[:, lo:hi_]
        outs.append(_dot(pooled.astype(BF16), wgrp_ref[gi]))
    y = jnp.concatenate(outs, axis=1) * ps_ref[...]
    _epilogue(x, y, mod_ref, g2_ref, rw_ref, x1_ref, h2_ref, lg_ref)


def _pool_bands(tm):
    te = tm + LANES
    t = jnp.arange(tm)[:, None]
    off = jnp.arange(te)[None, :] - POOL_HALO - t
    inside = jnp.arange(te)[None, :] < tm + 2 * POOL_HALO
    return jnp.stack([(inside & (off >= -(w // 2)) & (off <= w // 2 - 1)) for w in POOL_WINDOWS]).astype(BF16)


def _pool_layer(x, s, mod, g1, w_in, w_grp, p_scale, g2, rw2):
    bsz = x.shape[0]
    tm = POOL_TILE
    nb = tm // POOL_HALO
    last = s // POOL_HALO - 1
    out_specs, out_shape = _epilogue_specs(bsz, s, tm)
    halo = (1, N_CHUNK, POOL_HALO, LANES)
    bands = _pool_bands(tm)
    return pl.pallas_call(
        functools.partial(_pool_kernel, tm=tm, s=s),
        grid=(bsz, s // tm),
        in_specs=[
            pl.BlockSpec(halo, lambda b, i: (b, 0, jnp.maximum(i * nb - 1, 0), 0)),
            _x_spec(True, tm, D_MODEL),
            pl.BlockSpec(halo, lambda b, i: (b, 0, jnp.minimum((i + 1) * nb, last), 0)),
            _mod_spec(), _const_spec((1, D_MODEL)), _const_spec(w_in.shape), _const_spec(bands.shape),
            _const_spec(w_grp.shape), _const_spec((1, D_MODEL)), _const_spec((1, D_MODEL)),
            _const_spec(rw2.shape),
        ],
        out_specs=out_specs,
        out_shape=out_shape,
        compiler_params=_cparams("parallel", "arbitrary"),
        name="pool_mixer",
    )(x, x, x, mod, g1, w_in, bands, w_grp, p_scale, g2, rw2)


SORT_PHASE_BITS = 4


def _comes_first(a, ia, b, ib):
    return (a > b) | ((a == b) & (ia < ib))


def _sort_tokens(key_ref, tok_ref, sub, n):
    nv = key_ref.shape[0]
    vbits = nv.bit_length() - 1

    def exchange(ka, ta, kb, tb, inv):
        keep = _comes_first(ka, ta, kb, tb)
        if inv is not None:
            keep = jnp.logical_xor(keep, inv)
        return jnp.where(keep, ka, kb), jnp.where(keep, ta, tb), jnp.where(keep, kb, ka), jnp.where(keep, tb, ta)

    k = 2
    while k <= n:
        dists = [k >> d for d in range(1, k.bit_length())]
        sub_inv = None if k >= n else ((sub & (k >> vbits)) != 0 if k >= nv else False)

        for j in [d for d in dists if d >= nv]:
            js = j >> vbits
            lower = (sub & js) == 0
            later = jnp.logical_not(lower) if sub_inv is None else jnp.logical_xor(lower, jnp.logical_not(sub_inv))

            def rotate_stage(g, carry, js=js, lower=lower, later=later):
                for u in range(SUBLANES):
                    v = g * SUBLANES + u
                    x, t = key_ref[v], tok_ref[v]
                    px = jnp.where(lower, pltpu.roll(x, SUBLANES - js, 0), pltpu.roll(x, js, 0))
                    pt = jnp.where(lower, pltpu.roll(t, SUBLANES - js, 0), pltpu.roll(t, js, 0))
                    keep = jnp.logical_xor(_comes_first(x, t, px, pt), later)
                    key_ref[v] = jnp.where(keep, x, px)
                    tok_ref[v] = jnp.where(keep, t, pt)
                return carry

            lax.fori_loop(0, nv // SUBLANES, rotate_stage, 0)

        vreg_dists = [d for d in dists if d < nv]
        for p0 in range(0, len(vreg_dists), SORT_PHASE_BITS):
            phase = vreg_dists[p0:p0 + SORT_PHASE_BITS]
            lo = phase[-1].bit_length() - 1
            nb = len(phase)
            members = [m << lo for m in range(1 << nb)]

            def group_phase(q, carry, phase=phase, lo=lo, nb=nb, members=members, k=k, sub_inv=sub_inv):
                base = ((q >> lo) << (lo + nb)) | (q & ((1 << lo) - 1))
                if sub_inv is False:
                    bit = (base >> (k.bit_length() - 1)) & 1
                    inv = lax.broadcast(bit, (SUBLANES, LANES)) != 0
                else:
                    inv = sub_inv
                ks = {m: key_ref[base + m] for m in members}
                ts = {m: tok_ref[base + m] for m in members}
                for j in phase:
                    for m in members:
                        if not m & j:
                            ks[m], ts[m], ks[m | j], ts[m | j] = exchange(ks[m], ts[m], ks[m | j], ts[m | j], inv)
                for m in members:
                    key_ref[base + m] = ks[m]
                    tok_ref[base + m] = ts[m]
                return carry

            lax.fori_loop(0, nv >> nb, group_phase, 0)
        k *= 2


def _route_kernel(lg_ref, idx_ref, gate_ref, key_ref, tok_ref, *, s, cap):
    lg = lg_ref[...]
    ex = jnp.exp(lg - jnp.max(lg, axis=1, keepdims=True))
    aff = ex / jnp.sum(ex, axis=1, keepdims=True)
    nv = s // SUBLANES
    key_ref[...] = jnp.transpose(aff.reshape(LANES, s)).reshape(nv, SUBLANES, LANES)
    tok_ref[...] = (lax.broadcasted_iota(jnp.int32, (nv, SUBLANES, LANES), 0) * SUBLANES
                    + lax.broadcasted_iota(jnp.int32, (nv, SUBLANES, LANES), 1))
    sub = lax.broadcasted_iota(jnp.int32, (SUBLANES, LANES), 0)
    _sort_tokens(key_ref, tok_ref, sub, s)
    top_k = key_ref[:, 0, :]
    top_i = tok_ref[:, 0, :].astype(F32)
    pad = LANES - cap % LANES if cap % LANES else 0
    if pad:
        top_k = jnp.concatenate([top_k, jnp.zeros((pad, LANES), F32)], axis=0)
        top_i = jnp.concatenate([top_i, jnp.zeros((pad, LANES), F32)], axis=0)
    gate_ref[...] = jnp.transpose(top_k)[:, :cap]
    idx_ref[...] = jnp.transpose(top_i)[:, :cap].astype(jnp.int32)


def _route(logits_t, s):
    bsz = logits_t.shape[0]
    cap = CAPACITY_FACTOR * s // N_EXPERTS
    n_rows = bsz * N_EXPERTS
    assert n_rows == LANES and cap == s // SUBLANES
    return pl.pallas_call(
        functools.partial(_route_kernel, s=s, cap=cap),
        out_shape=[
            jax.ShapeDtypeStruct((n_rows, cap), jnp.int32),
            jax.ShapeDtypeStruct((n_rows, cap), F32),
        ],
        scratch_shapes=[pltpu.VMEM((s // SUBLANES, SUBLANES, LANES), F32),
                        pltpu.VMEM((s // SUBLANES, SUBLANES, LANES), jnp.int32)],
        compiler_params=pltpu.CompilerParams(vmem_limit_bytes=VMEM_LIMIT),
        name="expert_choice_route",
    )(logits_t)


MOE_TILES = 2
ROW_SLOTS = 3


def _moe_kernel(idx_cur, idx_nxt, idx_nxt2, h2_hbm, wg_hbm, wu_hbm, wd_hbm, o_ref,
                xin, stage_g, stage_u, stage_d, wg, wu, wd, sem_x, sem_w,
                *, layer, n_b, out_tiles, rows):
    e = pl.program_id(0)
    b = pl.program_id(1)
    step = e * n_b + b
    n_steps = N_EXPERTS * n_b
    n_rows = out_tiles * rows
    slot = step % ROW_SLOTS
    ahead = (step + ROW_SLOTS - 1) % ROW_SLOTS

    def row_copy(idx_ref, r, dst_slot):
        src_row = pl.multiple_of(idx_ref[0, 0, r] * N_CHUNK, N_CHUNK)
        return pltpu.make_async_copy(h2_hbm.at[pl.ds(src_row, N_CHUNK), :],
                                     xin.at[dst_slot, pl.ds(r * N_CHUNK, N_CHUNK), :],
                                     sem_x.at[dst_slot])

    def wait_rows(s_):
        pltpu.make_async_copy(h2_hbm.at[pl.ds(0, n_rows * N_CHUNK), :], xin.at[s_], sem_x.at[s_]).wait()

    def weight_copies(expert, ws):
        return [pltpu.make_async_copy(w.at[layer, expert], st.at[ws], sem_w.at[ws])
                for w, st in ((wg_hbm, stage_g), (wu_hbm, stage_u), (wd_hbm, stage_d))]

    @pl.when(step == 0)
    def _():
        for cp in weight_copies(0, 0):
            cp.start()

        def issue(r, carry):
            row_copy(idx_cur, r, 0).start()
            row_copy(idx_nxt, r, 1).start()
            return carry

        lax.fori_loop(0, n_rows, issue, 0)

    @pl.when(b == 0)
    def _():
        ws = e % 2
        for cp in weight_copies(e, ws):
            cp.wait()

        @pl.when(e + 1 < N_EXPERTS)
        def _():
            for cp in weight_copies(e + 1, 1 - ws):
                cp.start()

        rb = 128

        def cast(i, carry):
            rws = pl.ds(pl.multiple_of(i * rb, rb), rb)
            wg[rws, :] = stage_g[ws, rws, :].astype(BF16)
            wu[rws, :] = stage_u[ws, rws, :].astype(BF16)
            wd[rws, :] = stage_d[ws, rws, :].astype(BF16)
            return carry

        lax.fori_loop(0, D_MODEL // rb, cast, 0)

    wait_rows(slot)
    xs = xin.at[slot]
    x = jnp.concatenate([xs[pl.ds(j, n_rows, stride=N_CHUNK), :] for j in range(N_CHUNK)], axis=1).astype(BF16)
    hid = (jax.nn.silu(_dot(x, wg[...])) * _dot(x, wu[...])).astype(BF16)
    y = _dot(hid, wd[...])
    s2 = rows + ROW_PAD
    for n in range(out_tiles):
        for j in range(N_CHUNK):
            o_ref[0, 0, n, pl.ds(j * s2, rows), :] = y[n * rows:(n + 1) * rows, j * LANES:(j + 1) * LANES]
            o_ref[0, 0, n, pl.ds(j * s2 + rows, ROW_PAD), :] = jnp.zeros((ROW_PAD, LANES), F32)

    for r in range(n_rows):
        row_copy(idx_nxt2, r, ahead).start(priority=r % 2)

    @pl.when(step == n_steps - 1)
    def _():
        wait_rows(ahead)
        wait_rows((step + 1) % ROW_SLOTS)


def _moe_ffn(rows_glob, h2_tt, w_gate, w_up, w_down, layer, out_tiles, rows):
    n_b = rows_glob.shape[0] // N_EXPERTS
    n_rows = out_tiles * rows
    n_steps = N_EXPERTS * n_b
    s2 = rows + ROW_PAD
    wshape = w_gate.shape[2:]
    smem = functools.partial(pl.BlockSpec, memory_space=pltpu.SMEM)
    hbm = pl.BlockSpec(memory_space=pl.ANY)
    return pl.pallas_call(
        functools.partial(_moe_kernel, layer=layer, n_b=n_b, out_tiles=out_tiles, rows=rows),
        grid=(N_EXPERTS, n_b),
        in_specs=[
            smem((1, 1, n_rows), lambda e, b: (e * n_b + b, 0, 0)),
            smem((1, 1, n_rows), lambda e, b: (jnp.minimum(e * n_b + b + 1, n_steps - 1), 0, 0)),
            smem((1, 1, n_rows), lambda e, b: (jnp.minimum(e * n_b + b + 2, n_steps - 1), 0, 0)),
            hbm, hbm, hbm, hbm,
        ],
        out_specs=pl.BlockSpec((1, 1, out_tiles, N_CHUNK * s2, LANES), lambda e, b: (b, e, 0, 0, 0)),
        out_shape=jax.ShapeDtypeStruct((n_b, N_EXPERTS, out_tiles, N_CHUNK * s2, LANES), F32),
        scratch_shapes=[
            pltpu.VMEM((ROW_SLOTS, n_rows * N_CHUNK, LANES), F32),
            pltpu.VMEM((2,) + wshape, F32), pltpu.VMEM((2,) + wshape, F32), pltpu.VMEM((2,) + wshape, F32),
            pltpu.VMEM(wshape, BF16), pltpu.VMEM(wshape, BF16), pltpu.VMEM(wshape, BF16),
            pltpu.SemaphoreType.DMA((ROW_SLOTS,)), pltpu.SemaphoreType.DMA((2,)),
        ],
        compiler_params=_cparams("arbitrary", "arbitrary"),
        name="expert_ffn",
    )(rows_glob, rows_glob, rows_glob, h2_tt, w_gate, w_up, w_down)


def _global_rows(idx, bsz, s, tiles):
    cap = idx.shape[-1]
    glob = idx.reshape(bsz, N_EXPERTS, cap) + (jnp.arange(bsz, dtype=jnp.int32) * s)[:, None, None]
    return _expert_major(glob, bsz, tiles)


def _expert_major(per_row, bsz, tiles):
    cap = per_row.shape[-1]
    return jnp.swapaxes(per_row.reshape(bsz, N_EXPERTS, cap), 0, 1).reshape(
        N_EXPERTS * (bsz // tiles), 1, tiles * cap)


COMBINE_UNROLL = 8
COMBINE_EXPERTS = 4


def _combine_kernel(idx_ref, gate_ref, x_ref, y_ref, mod_ref, o_ref, acc, *, s, sp, cap, s2, n_seg, to_rows):
    eg = pl.program_id(1)
    base = pl.program_id(0) * cap if n_seg > 1 else 0

    @pl.when(eg == 0)
    def _():
        for j in range(N_CHUNK):
            acc[pl.ds(j * sp, s), :] = x_ref[0, j]
            acc[pl.ds(j * sp + s, ROW_PAD), :] = jnp.zeros((ROW_PAD, LANES), F32)

    g2 = mod_ref[0, 5]

    for k in range(COMBINE_EXPERTS):
        y = y_ref.at[0, k, 0]

        def body(i, carry):
            new = []
            for u in range(COMBINE_UNROLL):
                r = i * COMBINE_UNROLL + u
                t = idx_ref[0, 0, k * cap + r]
                yrow = y[pl.ds(base + r, N_CHUNK, stride=s2), :]
                cur = acc[pl.ds(t, N_CHUNK, stride=sp), :]
                new.append((t, cur + (gate_ref[0, 0, k * cap + r] * g2) * yrow))
            for t, val in new:
                acc[pl.ds(t, N_CHUNK, stride=sp), :] = val
            return carry

        lax.fori_loop(0, cap // COMBINE_UNROLL, body, 0)

    @pl.when(eg == N_EXPERTS // COMBINE_EXPERTS - 1)
    def _():
        for j in range(N_CHUNK):
            if to_rows:
                o_ref[0, :, j * LANES:(j + 1) * LANES] = acc[pl.ds(j * sp, s), :]
            else:
                o_ref[0, j] = acc[pl.ds(j * sp, s), :]


def _combine(idx, gate, x1cm, ycm, mod, s, to_rows):
    bsz = x1cm.shape[0]
    sp = s + ROW_PAD
    cap = idx.shape[-1]
    n_b, _, tiles, rows8, _ = ycm.shape
    s2 = rows8 // N_CHUNK
    n_seg = bsz // (n_b * tiles)
    eg = COMBINE_EXPERTS
    y_spec = pl.BlockSpec((1, eg, 1, rows8, LANES), lambda b, g: (b // tiles // n_seg, g, b % tiles, 0, 0))
    smem = functools.partial(pl.BlockSpec, memory_space=pltpu.SMEM)
    cm_spec = pl.BlockSpec((1, N_CHUNK, s, LANES), lambda b, g: (b, 0, 0, 0))
    if to_rows:
        out_spec = pl.BlockSpec((1, s, D_MODEL), lambda b, g: (b, 0, 0))
        out_shape = jax.ShapeDtypeStruct((bsz, s, D_MODEL), F32)
    else:
        out_spec = cm_spec
        out_shape = jax.ShapeDtypeStruct((bsz, N_CHUNK, s, LANES), F32)
    n_grp = N_EXPERTS // eg
    return pl.pallas_call(
        functools.partial(_combine_kernel, s=s, sp=sp, cap=cap, s2=s2, n_seg=n_seg, to_rows=to_rows),
        grid=(bsz, n_grp),
        in_specs=[
            smem((1, 1, eg * cap), lambda b, g: (b * n_grp + g, 0, 0)),
            smem((1, 1, eg * cap), lambda b, g: (b * n_grp + g, 0, 0)),
            cm_spec,
            y_spec,
            pl.BlockSpec((1, 6, N_CHUNK, LANES), lambda b, g: (b, 0, 0, 0)),
        ],
        out_specs=out_spec,
        out_shape=out_shape,
        scratch_shapes=[pltpu.VMEM((N_CHUNK * sp, LANES), F32)],
        compiler_params=_cparams("parallel", "arbitrary"),
        name="combine_rows" if to_rows else "combine_cm",
    )(idx.reshape(bsz * n_grp, 1, eg * cap), gate.reshape(bsz * n_grp, 1, eg * cap),
      x1cm, ycm, mod.reshape(bsz, 6, N_CHUNK, LANES))


def _rope_tables(n):
    rows = n // GRID_W
    row = jnp.repeat(jnp.arange(rows, dtype=F32), GRID_W)
    col = jnp.tile(jnp.arange(GRID_W, dtype=F32), rows)
    n_freq = HEAD_DIM // 4
    inv = jnp.power(ROPE_THETA, -jnp.arange(n_freq, dtype=F32) / n_freq)
    ang = jnp.concatenate([row[:, None] * inv, col[:, None] * inv], axis=-1)
    cos, sin = jnp.cos(ang), jnp.sin(ang)
    return jnp.concatenate([cos, cos], axis=-1), jnp.concatenate([-sin, sin], axis=-1)


def _router_rows(router_w):
    wt = router_w.T
    hi = wt.astype(BF16)
    lo = (wt - hi.astype(F32)).astype(BF16)
    return jnp.concatenate([hi, lo], axis=0)


def kernel(x, c, ctx, c_ctx, mod_w, mod_b, norm1_g, norm2_g, router_w, exp_w_gate, exp_w_up, exp_w_down,
           g_w_in, g_v_norm_g, g_spatial_w, g_spatial_b, g_w_out, att_w_in, att_q_norm_g, att_k_norm_g,
           att_w_out, p_w_in, p_w_group, p_scale):
    bsz, s, d = x.shape
    s_ctx = ctx.shape[1]
    depth = mod_w.shape[0]
    assert d == D_MODEL and bsz * N_EXPERTS == LANES and bsz % MOE_TILES == 0
    assert s % GMLP_TILE == 0 and s_ctx % QKV_TILE == 0 and (s + s_ctx) % ATTN_KEY_BLOCK == 0

    cc = jnp.zeros((16, d), F32).at[:bsz].set(c).at[bsz].set(c_ctx)
    mods = _modulation(cc, mod_w, mod_b).reshape(depth, 16, 6, d)

    x_cur, x_cm = x, False
    c_cur, c_cm = ctx, False
    for i in range(depth):
        kind, j = i % 3, i // 3
        update_ctx = any(l % 3 == 1 for l in range(i + 1, depth))
        last = i == depth - 1
        mod_lat = mods[i, :bsz]
        mod_ctx = jnp.broadcast_to(mods[i, bsz][None], (bsz, 6, d))
        g1 = norm1_g[i][None]
        g2 = norm2_g[i][None]
        rw2 = _router_rows(router_w[i])

        ctx_out = None
        if kind == 0:
            s_bias = jnp.repeat(g_spatial_b[j].T, LANES, axis=1)
            gm = functools.partial(
                _gmlp_layer, g1=g1, w_in=g_w_in[j].astype(BF16), v_g=g_v_norm_g[j][None],
                s_w=g_spatial_w[j].astype(BF16), s_bias=s_bias, w_out=g_w_out[j].astype(BF16),
                g2=g2, rw2=rw2)
            x1, h2, lg = gm(x_cur, x_cm, s, mod_lat)
            if update_ctx:
                ctx_out = gm(c_cur, c_cm, s_ctx, mod_ctx)
        elif kind == 1:
            assert not update_ctx
            cos, sin = _rope_tables(s)
            w_in = att_w_in[j].astype(BF16)
            qg, kg = att_q_norm_g[j][None], att_k_norm_g[j][None]
            q, k, v = _qkv_proj(x_cur, x_cm, s, mod_lat, g1, w_in, qg, kg, cos, sin, True)
            kc, vc = _qkv_proj(c_cur, c_cm, s_ctx, mod_ctx, g1, w_in, qg, kg,
                               jnp.ones((s_ctx, HEAD_DIM), F32), jnp.zeros((s_ctx, HEAD_DIM), F32), False)
            k_all = jnp.concatenate([kc, k], axis=1)
            v_all = jnp.concatenate([vc, v], axis=1)
            assert x_cm
            x1, h2, lg = _attn_layer(x_cur, s, q, k_all, v_all, mod_lat, att_w_out[j].astype(BF16), g2, rw2)
        else:
            assert not update_ctx and x_cm
            x1, h2, lg = _pool_layer(x_cur, s, mod_lat, g1, p_w_in[j].astype(BF16),
                                     p_w_group[j].astype(BF16), p_scale[j][None], g2, rw2)

        experts = functools.partial(_moe_ffn, w_gate=exp_w_gate, w_up=exp_w_up, w_down=exp_w_down, layer=i)
        idx, gate = _route(lg, s)
        ys = experts(_global_rows(idx, bsz, s, MOE_TILES), h2.reshape(bsz * s * N_CHUNK, LANES),
                     out_tiles=MOE_TILES, rows=idx.shape[-1])
        x_cur = _combine(idx, gate, x1, ys, mod_lat, s, last)
        x_cm = True
        if ctx_out is not None:
            c1, hc2, lgc = ctx_out
            idx_c, gate_c = _route(lgc, s_ctx)
            ys_c = experts(_global_rows(idx_c, bsz, s_ctx, bsz), hc2.reshape(bsz * s_ctx * N_CHUNK, LANES),
                           out_tiles=1, rows=bsz * idx_c.shape[-1])
            c_cur = _combine(idx_c, gate_c, c1, ys_c, mod_ctx, s_ctx, False)
            c_cm = True
    return x_cur
```

```python
import functools

import jax
import jax.numpy as jnp
from jax import lax
from jax.experimental import pallas as pl
from jax.experimental.pallas import tpu as pltpu

F32 = jnp.float32
BF16 = jnp.bfloat16

D_MODEL = 1024
LANES = 128
SUBLANES = 8
N_CHUNK = D_MODEL // LANES
ROW_PAD = 8
EPS = 1e-6
GRID_W = 64
CHUNK = 128
A_GROUPS = 8
HEAD_DIM = 128
N_HEADS = 8
N_KV_HEADS = 2
Q_PER_KV = N_HEADS // N_KV_HEADS
ROPE_THETA = 10000.0
LOG2_E = 1.4426950408889634
GMLP_TILE = 1024
GMLP_SUB_TILE = 1024
QKV_TILE = 256
ATTN_Q_TILE = 512
ATTN_HEAD_GROUP = Q_PER_KV
ATTN_KEY_BLOCK = 768
POOL_TILE = 512
POOL_WINDOWS = (2, 4, 8, 16)
POOL_GROUP_DIM = D_MODEL // len(POOL_WINDOWS)
POOL_HALO = 8
N_EXPERTS = 16
CAPACITY_FACTOR = 2
VMEM_LIMIT = 56 * 1024 * 1024


def _cparams(*sem):
    return pltpu.CompilerParams(dimension_semantics=sem, vmem_limit_bytes=VMEM_LIMIT)


def _dot(a, b):
    return jnp.dot(a, b, preferred_element_type=F32)


def _dot_nt(a, b):
    return lax.dot_general(a, b, (((1,), (1,)), ((), ())), preferred_element_type=F32)


def _chunks_to_tile(ref, lead, rows):
    return jnp.concatenate([ref[lead + (j, rows)] for j in range(N_CHUNK)], axis=-1)


def _tile_to_chunks(ref, lead, rows, val):
    for j in range(N_CHUNK):
        ref[lead + (j, rows)] = val[:, j * LANES:(j + 1) * LANES]


def _rms_mod(x, g, shift, scale):
    y = x * lax.rsqrt(jnp.mean(x * x, axis=-1, keepdims=True) + EPS)
    return (y * g) * (1.0 + scale) + shift


def _mod_kernel(c_ref, w_ref, b_ref, o_ref):
    a = jax.nn.silu(c_ref[...]).astype(BF16)
    o_ref[0] = _dot(a, w_ref[0].astype(BF16)) + b_ref[0]


def _modulation(cc, mod_w, mod_b):
    depth, d, n = mod_w.shape
    tn = 1536
    return pl.pallas_call(
        _mod_kernel,
        grid=(depth, n // tn),
        in_specs=[
            pl.BlockSpec((16, d), lambda l, j: (0, 0)),
            pl.BlockSpec((1, d, tn), lambda l, j: (l, 0, j)),
            pl.BlockSpec((1, 1, tn), lambda l, j: (l, 0, j)),
        ],
        out_specs=pl.BlockSpec((1, 16, tn), lambda l, j: (l, 0, j)),
        out_shape=jax.ShapeDtypeStruct((depth, 16, n), F32),
        compiler_params=_cparams("parallel", "parallel"),
        name="adaln_modulation",
    )(cc, mod_w, mod_b.reshape(depth, 1, n))


def _load_x(x_ref, x_cm, r0=0, n=None):
    rows = slice(None) if n is None else slice(r0, r0 + n)
    if x_cm:
        return _chunks_to_tile(x_ref, (0,), rows)
    return x_ref[0, rows]


def _epilogue(x, y, mod_ref, g2_ref, rw_ref, x1_ref, h2_ref, lg_ref, r0=0):
    n = x.shape[0]
    x1 = x + mod_ref[0, 2:3, :] * y
    _tile_to_chunks(x1_ref, (0,), slice(r0, r0 + n), x1)
    h2 = _rms_mod(x1, g2_ref[...], mod_ref[0, 3:4, :], mod_ref[0, 4:5, :])
    for j in range(N_CHUNK):
        h2_ref[0, pl.ds(r0 * N_CHUNK + j, n, stride=N_CHUNK), :] = h2[:, j * LANES:(j + 1) * LANES]
    hi = h2.astype(BF16)
    lo = (h2 - hi.astype(F32)).astype(BF16)
    rw = rw_ref[...]
    a = _dot_nt(rw, hi)
    b = _dot_nt(rw[:N_EXPERTS], lo)
    lg_ref[0, :, r0:r0 + n] = a[:N_EXPERTS] + a[N_EXPERTS:] + b


def _x_spec(x_cm, tm, d):
    if x_cm:
        return pl.BlockSpec((1, N_CHUNK, tm, LANES), lambda b, i: (b, 0, i, 0))
    return pl.BlockSpec((1, tm, d), lambda b, i: (b, i, 0))


def _const_spec(shape):
    nd = len(shape)
    return pl.BlockSpec(shape, lambda b, i: (0,) * nd)


def _epilogue_specs(bsz, s, tm):
    cm_spec = pl.BlockSpec((1, N_CHUNK, tm, LANES), lambda b, i: (b, 0, i, 0))
    tt_spec = pl.BlockSpec((1, tm * N_CHUNK, LANES), lambda b, i: (b, i, 0))
    out_specs = [cm_spec, tt_spec, pl.BlockSpec((1, N_EXPERTS, tm), lambda b, i: (b, 0, i))]
    out_shape = [
        jax.ShapeDtypeStruct((bsz, N_CHUNK, s, LANES), F32),
        jax.ShapeDtypeStruct((bsz, s * N_CHUNK, LANES), F32),
        jax.ShapeDtypeStruct((bsz, N_EXPERTS, s), F32),
    ]
    return out_specs, out_shape


def _mod_spec():
    return pl.BlockSpec((1, 6, D_MODEL), lambda b, i: (b, 0, 0))


def _gmlp_kernel(x_ref, mod_ref, g1_ref, win_ref, vg_ref, sw_ref, sb_ref, wout_ref, g2_ref, rw_ref,
                 x1_ref, h2_ref, lg_ref, *, x_cm, tm):
    sub = min(tm, GMLP_SUB_TILE)
    nc = sub // CHUNK
    for r0 in range(0, tm, sub):
        x = _load_x(x_ref, x_cm, r0, sub)
        h = _rms_mod(x, g1_ref[...], mod_ref[0, 0:1, :], mod_ref[0, 1:2, :]).astype(BF16)
        z = jax.nn.gelu(_dot(h, win_ref[...]))
        u = z[:, :D_MODEL]
        v = z[:, D_MODEL:]
        mu = jnp.mean(v, axis=-1, keepdims=True)
        vc = v - mu
        v = (vc * lax.rsqrt(jnp.mean(vc * vc, axis=-1, keepdims=True) + EPS)) * vg_ref[...]
        vb = v.astype(BF16)
        mixed = []
        for g in range(A_GROUPS):
            cols = jnp.concatenate(
                [vb[c * CHUNK:(c + 1) * CHUNK, g * LANES:(g + 1) * LANES] for c in range(nc)], axis=1)
            mixed.append(_dot(sw_ref[g], cols))
        sv = jnp.concatenate(
            [jnp.concatenate([mixed[g][:, c * LANES:(c + 1) * LANES] for g in range(A_GROUPS)], axis=1)
             + sb_ref[...] for c in range(nc)], axis=0)
        y = _dot((u * sv).astype(BF16), wout_ref[...])
        _epilogue(x, y, mod_ref, g2_ref, rw_ref, x1_ref, h2_ref, lg_ref, r0)


def _gmlp_layer(x, x_cm, s, mod, g1, w_in, v_g, s_w, s_bias, w_out, g2, rw2):
    bsz = x.shape[0]
    tm = min(GMLP_TILE, s)
    out_specs, out_shape = _epilogue_specs(bsz, s, tm)
    return pl.pallas_call(
        functools.partial(_gmlp_kernel, x_cm=x_cm, tm=tm),
        grid=(bsz, s // tm),
        in_specs=[
            _x_spec(x_cm, tm, D_MODEL), _mod_spec(), _const_spec((1, D_MODEL)),
            _const_spec(w_in.shape), _const_spec((1, D_MODEL)), _const_spec(s_w.shape),
            _const_spec(s_bias.shape), _const_spec(w_out.shape), _const_spec((1, D_MODEL)),
            _const_spec(rw2.shape),
        ],
        out_specs=out_specs,
        out_shape=out_shape,
        compiler_params=_cparams("parallel", "arbitrary"),
        name="gmlp_mixer",
    )(x, mod, g1, w_in, v_g, s_w, s_bias, w_out, g2, rw2)


def _qkv_kernel(x_ref, mod_ref, g1_ref, w_ref, qg_ref, kg_ref, cos_ref, sin_ref, *out_refs,
                x_cm, want_q):
    x = _load_x(x_ref, x_cm)
    h = _rms_mod(x, g1_ref[...], mod_ref[0, 0:1, :], mod_ref[0, 1:2, :]).astype(BF16)
    qkv = _dot(h, w_ref[...])
    cos = cos_ref[...]
    sin = sin_ref[...]

    def norm_rope(t, g):
        t = (t * lax.rsqrt(jnp.mean(t * t, axis=-1, keepdims=True) + EPS)) * g
        return t * cos + pltpu.roll(t, HEAD_DIM // 2, 1) * sin

    qd = N_HEADS * HEAD_DIM
    kvd = N_KV_HEADS * HEAD_DIM
    if want_q:
        q_ref, k_ref, v_ref = out_refs
        for hd in range(N_HEADS):
            qh = norm_rope(qkv[:, hd * HEAD_DIM:(hd + 1) * HEAD_DIM], qg_ref[...])
            q_ref[0, :, hd * HEAD_DIM:(hd + 1) * HEAD_DIM] = (qh * (HEAD_DIM ** -0.5 * LOG2_E)).astype(BF16)
    else:
        k_ref, v_ref = out_refs
    for hd in range(N_KV_HEADS):
        kh = norm_rope(qkv[:, qd + hd * HEAD_DIM:qd + (hd + 1) * HEAD_DIM], kg_ref[...])
        k_ref[0, :, hd * HEAD_DIM:(hd + 1) * HEAD_DIM] = kh.astype(BF16)
    v_ref[0] = qkv[:, qd + kvd:].astype(BF16)


def _qkv_proj(x, x_cm, s, mod, g1, w_in, q_g, k_g, cos, sin, want_q):
    bsz = x.shape[0]
    tm = min(QKV_TILE, s)
    qd = N_HEADS * HEAD_DIM
    kvd = N_KV_HEADS * HEAD_DIM
    kv_spec = pl.BlockSpec((1, tm, kvd), lambda b, i: (b, i, 0))
    kv_shape = jax.ShapeDtypeStruct((bsz, s, kvd), BF16)
    out_specs = [kv_spec, kv_spec]
    out_shape = [kv_shape, kv_shape]
    if want_q:
        out_specs = [pl.BlockSpec((1, tm, qd), lambda b, i: (b, i, 0))] + out_specs
        out_shape = [jax.ShapeDtypeStruct((bsz, s, qd), BF16)] + out_shape
    return pl.pallas_call(
        functools.partial(_qkv_kernel, x_cm=x_cm, want_q=want_q),
        grid=(bsz, s // tm),
        in_specs=[
            _x_spec(x_cm, tm, D_MODEL), _mod_spec(), _const_spec((1, D_MODEL)),
            _const_spec(w_in.shape), _const_spec((1, HEAD_DIM)), _const_spec((1, HEAD_DIM)),
            pl.BlockSpec((tm, HEAD_DIM), lambda b, i: (i, 0)),
            pl.BlockSpec((tm, HEAD_DIM), lambda b, i: (i, 0)),
        ],
        out_specs=out_specs,
        out_shape=out_shape,
        compiler_params=_cparams("parallel", "arbitrary"),
        name="qkv_proj",
    )(x, mod, g1, w_in, q_g, k_g, cos, sin)


def _attn_kernel(x_ref, q_ref, k_ref, v_ref, mod_ref, wout_ref, g2_ref, rw_ref,
                 x1_ref, h2_ref, lg_ref, *, tq):
    x = _load_x(x_ref, True)
    heads = [None] * N_HEADS
    sk = k_ref.shape[1]
    ones_col = jnp.where(lax.broadcasted_iota(jnp.int32, (sk, HEAD_DIM), 1) == 0, 1.0, 0.0).astype(BF16)
    for h0 in range(0, N_HEADS, ATTN_HEAD_GROUP):
        kh = h0 // Q_PER_KV
        k = k_ref[0, :, kh * HEAD_DIM:(kh + 1) * HEAD_DIM]
        v = jnp.concatenate([v_ref[0, :, kh * HEAD_DIM:(kh + 1) * HEAD_DIM], ones_col], axis=1)
        qg = jnp.concatenate(
            [q_ref[0, :, (h0 + g) * HEAD_DIM:(h0 + g + 1) * HEAD_DIM] for g in range(ATTN_HEAD_GROUP)], axis=0)
        m = jnp.full((ATTN_HEAD_GROUP * tq, 1), -jnp.inf, F32)
        o = jnp.zeros((ATTN_HEAD_GROUP * tq, 2 * HEAD_DIM), F32)
        for kb in range(0, sk, ATTN_KEY_BLOCK):
            sc = _dot_nt(qg, k[kb:kb + ATTN_KEY_BLOCK])
            m_new = jnp.maximum(m, jnp.max(sc, axis=-1, keepdims=True))
            p = jnp.exp2(sc - m_new).astype(BF16)
            o = o * jnp.exp2(m - m_new) + _dot(p, v[kb:kb + ATTN_KEY_BLOCK])
            m = m_new
        o = o[:, :HEAD_DIM] / o[:, HEAD_DIM:HEAD_DIM + 1]
        for g in range(ATTN_HEAD_GROUP):
            heads[h0 + g] = o[g * tq:(g + 1) * tq].astype(BF16)
    y = _dot(jnp.concatenate(heads, axis=1), wout_ref[...])
    _epilogue(x, y, mod_ref, g2_ref, rw_ref, x1_ref, h2_ref, lg_ref)


def _attn_layer(x, s, q, k_all, v_all, mod, w_out, g2, rw2):
    bsz = x.shape[0]
    tq = ATTN_Q_TILE
    sk = k_all.shape[1]
    qd = N_HEADS * HEAD_DIM
    kvd = N_KV_HEADS * HEAD_DIM
    out_specs, out_shape = _epilogue_specs(bsz, s, tq)
    return pl.pallas_call(
        functools.partial(_attn_kernel, tq=tq),
        grid=(bsz, s // tq),
        in_specs=[
            _x_spec(True, tq, D_MODEL),
            pl.BlockSpec((1, tq, qd), lambda b, i: (b, i, 0)),
            pl.BlockSpec((1, sk, kvd), lambda b, i: (b, 0, 0)),
            pl.BlockSpec((1, sk, kvd), lambda b, i: (b, 0, 0)),
            _mod_spec(), _const_spec(w_out.shape), _const_spec((1, D_MODEL)), _const_spec(rw2.shape),
        ],
        out_specs=out_specs,
        out_shape=out_shape,
        compiler_params=_cparams("parallel", "arbitrary"),
        name="gqa_attention",
    )(x, q, k_all, v_all, mod, w_out, g2, rw2)


def _pool_kernel(xp_ref, x_ref, xn_ref, mod_ref, g1_ref, win_ref, band_ref, wgrp_ref, ps_ref, g2_ref,
                 rw_ref, x1_ref, h2_ref, lg_ref, *, tm, s):
    i = pl.program_id(1)
    x = _load_x(x_ref, True)
    xe = jnp.concatenate([_load_x(xp_ref, True), x, _load_x(xn_ref, True)], axis=0)
    h = _rms_mod(xe, g1_ref[...], mod_ref[0, 0:1, :], mod_ref[0, 1:2, :]).astype(BF16)
    z = _dot(h, win_ref[...])
    te = band_ref.shape[2]
    zp = z[:POOL_HALO] * jnp.where(i > 0, 1.0, 0.0)
    zn = z[POOL_HALO + tm:] * jnp.where(i < pl.num_programs(1) - 1, 1.0, 0.0)
    zc = z[POOL_HALO:POOL_HALO + tm]
    zb = jnp.concatenate([zp, zc, zn, jnp.zeros((te - tm - 2 * POOL_HALO, D_MODEL), F32)], axis=0).astype(BF16)
    tcol = i * tm + lax.broadcasted_iota(jnp.int32, (tm, 1), 0)
    outs = []
    for gi, w in enumerate(POOL_WINDOWS):
        half = w // 2
        cnt = (jnp.minimum(tcol + half - 1, s - 1) - jnp.maximum(tcol - half, 0) + 1).astype(F32)
        lo, hi_ = gi * POOL_GROUP_DIM, (gi + 1) * POOL_GROUP_DIM
        pooled = _dot(band_ref[gi], zb[:, lo:hi_]) / cnt - zc[:, lo:hi_]
        outs.append(_dot(pooled.astype(BF16), wgrp_ref[gi]))
    y = jnp.concatenate(outs, axis=1) * ps_ref[...]
    _epilogue(x, y, mod_ref, g2_ref, rw_ref, x1_ref, h2_ref, lg_ref)


def _pool_bands(tm):
    te = tm + LANES
    t = jnp.arange(tm)[:, None]
    off = jnp.arange(te)[None, :] - POOL_HALO - t
    inside = jnp.arange(te)[None, :] < tm + 2 * POOL_HALO
    return jnp.stack([(inside & (off >= -(w // 2)) & (off <= w // 2 - 1)) for w in POOL_WINDOWS]).astype(BF16)


def _pool_layer(x, s, mod, g1, w_in, w_grp, p_scale, g2, rw2):
    bsz = x.shape[0]
    tm = POOL_TILE
    nb = tm // POOL_HALO
    last = s // POOL_HALO - 1
    out_specs, out_shape = _epilogue_specs(bsz, s, tm)
    halo = (1, N_CHUNK, POOL_HALO, LANES)
    bands = _pool_bands(tm)
    return pl.pallas_call(
        functools.partial(_pool_kernel, tm=tm, s=s),
        grid=(bsz, s // tm),
        in_specs=[
            pl.BlockSpec(halo, lambda b, i: (b, 0, jnp.maximum(i * nb - 1, 0), 0)),
            _x_spec(True, tm, D_MODEL),
            pl.BlockSpec(halo, lambda b, i: (b, 0, jnp.minimum((i + 1) * nb, last), 0)),
            _mod_spec(), _const_spec((1, D_MODEL)), _const_spec(w_in.shape), _const_spec(bands.shape),
            _const_spec(w_grp.shape), _const_spec((1, D_MODEL)), _const_spec((1, D_MODEL)),
            _const_spec(rw2.shape),
        ],
        out_specs=out_specs,
        out_shape=out_shape,
        compiler_params=_cparams("parallel", "arbitrary"),
        name="pool_mixer",
    )(x, x, x, mod, g1, w_in, bands, w_grp, p_scale, g2, rw2)


SORT_PHASE_BITS = 4


def _comes_first(a, ia, b, ib):
    return (a > b) | ((a == b) & (ia < ib))


def _sort_tokens(key_ref, tok_ref, sub, n):
    nv = key_ref.shape[0]
    vbits = nv.bit_length() - 1

    def exchange(ka, ta, kb, tb, inv):
        keep = _comes_first(ka, ta, kb, tb)
        if inv is not None:
            keep = jnp.logical_xor(keep, inv)
        return jnp.where(keep, ka, kb), jnp.where(keep, ta, tb), jnp.where(keep, kb, ka), jnp.where(keep, tb, ta)

    k = 2
    while k <= n:
        dists = [k >> d for d in range(1, k.bit_length())]
        sub_inv = None if k >= n else ((sub & (k >> vbits)) != 0 if k >= nv else False)

        for j in [d for d in dists if d >= nv]:
            js = j >> vbits
            lower = (sub & js) == 0
            later = jnp.logical_not(lower) if sub_inv is None else jnp.logical_xor(lower, jnp.logical_not(sub_inv))

            def rotate_stage(g, carry, js=js, lower=lower, later=later):
                for u in range(SUBLANES):
                    v = g * SUBLANES + u
                    x, t = key_ref[v], tok_ref[v]
                    px = jnp.where(lower, pltpu.roll(x, SUBLANES - js, 0), pltpu.roll(x, js, 0))
                    pt = jnp.where(lower, pltpu.roll(t, SUBLANES - js, 0), pltpu.roll(t, js, 0))
                    keep = jnp.logical_xor(_comes_first(x, t, px, pt), later)
                    key_ref[v] = jnp.where(keep, x, px)
                    tok_ref[v] = jnp.where(keep, t, pt)
                return carry

            lax.fori_loop(0, nv // SUBLANES, rotate_stage, 0)

        vreg_dists = [d for d in dists if d < nv]
        for p0 in range(0, len(vreg_dists), SORT_PHASE_BITS):
            phase = vreg_dists[p0:p0 + SORT_PHASE_BITS]
            lo = phase[-1].bit_length() - 1
            nb = len(phase)
            members = [m << lo for m in range(1 << nb)]

            def group_phase(q, carry, phase=phase, lo=lo, nb=nb, members=members, k=k, sub_inv=sub_inv):
                base = ((q >> lo) << (lo + nb)) | (q & ((1 << lo) - 1))
                if sub_inv is False:
                    bit = (base >> (k.bit_length() - 1)) & 1
                    inv = lax.broadcast(bit, (SUBLANES, LANES)) != 0
                else:
                    inv = sub_inv
                ks = {m: key_ref[base + m] for m in members}
                ts = {m: tok_ref[base + m] for m in members}
                for j in phase:
                    for m in members:
                        if not m & j:
                            ks[m], ts[m], ks[m | j], ts[m | j] = exchange(ks[m], ts[m], ks[m | j], ts[m | j], inv)
                for m in members:
                    key_ref[base + m] = ks[m]
                    tok_ref[base + m] = ts[m]
                return carry

            lax.fori_loop(0, nv >> nb, group_phase, 0)
        k *= 2


def _route_kernel(lg_ref, idx_ref, gate_ref, key_ref, tok_ref, *, s, cap):
    lg = lg_ref[...]
    ex = jnp.exp(lg - jnp.max(lg, axis=1, keepdims=True))
    aff = ex / jnp.sum(ex, axis=1, keepdims=True)
    nv = s // SUBLANES
    key_ref[...] = jnp.transpose(aff.reshape(LANES, s)).reshape(nv, SUBLANES, LANES)
    tok_ref[...] = (lax.broadcasted_iota(jnp.int32, (nv, SUBLANES, LANES), 0) * SUBLANES
                    + lax.broadcasted_iota(jnp.int32, (nv, SUBLANES, LANES), 1))
    sub = lax.broadcasted_iota(jnp.int32, (SUBLANES, LANES), 0)
    _sort_tokens(key_ref, tok_ref, sub, s)
    top_k = key_ref[:, 0, :]
    top_i = tok_ref[:, 0, :].astype(F32)
    pad = LANES - cap % LANES if cap % LANES else 0
    if pad:
        top_k = jnp.concatenate([top_k, jnp.zeros((pad, LANES), F32)], axis=0)
        top_i = jnp.concatenate([top_i, jnp.zeros((pad, LANES), F32)], axis=0)
    gate_ref[...] = jnp.transpose(top_k)[:, :cap]
    idx_ref[...] = jnp.transpose(top_i)[:, :cap].astype(jnp.int32)


def _route(logits_t, s):
    bsz = logits_t.shape[0]
    cap = CAPACITY_FACTOR * s // N_EXPERTS
    n_rows = bsz * N_EXPERTS
    assert n_rows == LANES and cap == s // SUBLANES
    return pl.pallas_call(
        functools.partial(_route_kernel, s=s, cap=cap),
        out_shape=[
            jax.ShapeDtypeStruct((n_rows, cap), jnp.int32),
            jax.ShapeDtypeStruct((n_rows, cap), F32),
        ],
        scratch_shapes=[pltpu.VMEM((s // SUBLANES, SUBLANES, LANES), F32),
                        pltpu.VMEM((s // SUBLANES, SUBLANES, LANES), jnp.int32)],
        compiler_params=pltpu.CompilerParams(vmem_limit_bytes=VMEM_LIMIT),
        name="expert_choice_route",
    )(logits_t)


MOE_TILES = 2
ROW_SLOTS = 3


def _moe_kernel(idx_cur, idx_nxt, idx_nxt2, h2_hbm, wg_hbm, wu_hbm, wd_hbm, o_ref,
                xin, stage_g, stage_u, stage_d, wg, wu, wd, sem_x, sem_w,
                *, layer, n_b, out_tiles, rows):
    e = pl.program_id(0)
    b = pl.program_id(1)
    step = e * n_b + b
    n_steps = N_EXPERTS * n_b
    n_rows = out_tiles * rows
    slot = step % ROW_SLOTS
    ahead = (step + ROW_SLOTS - 1) % ROW_SLOTS

    def row_copy(idx_ref, r, dst_slot):
        src_row = pl.multiple_of(idx_ref[0, 0, r] * N_CHUNK, N_CHUNK)
        return pltpu.make_async_copy(h2_hbm.at[pl.ds(src_row, N_CHUNK), :],
                                     xin.at[dst_slot, pl.ds(r * N_CHUNK, N_CHUNK), :],
                                     sem_x.at[dst_slot])

    def wait_rows(s_):
        pltpu.make_async_copy(h2_hbm.at[pl.ds(0, n_rows * N_CHUNK), :], xin.at[s_], sem_x.at[s_]).wait()

    def weight_copies(expert, ws):
        return [pltpu.make_async_copy(w.at[layer, expert], st.at[ws], sem_w.at[ws])
                for w, st in ((wg_hbm, stage_g), (wu_hbm, stage_u), (wd_hbm, stage_d))]

    @pl.when(step == 0)
    def _():
        for cp in weight_copies(0, 0):
            cp.start()

        def issue(r, carry):
            row_copy(idx_cur, r, 0).start()
            row_copy(idx_nxt, r, 1).start()
            return carry

        lax.fori_loop(0, n_rows, issue, 0)

    @pl.when(b == 0)
    def _():
        ws = e % 2
        for cp in weight_copies(e, ws):
            cp.wait()

        @pl.when(e + 1 < N_EXPERTS)
        def _():
            for cp in weight_copies(e + 1, 1 - ws):
                cp.start()

        rb = 128

        def cast(i, carry):
            rws = pl.ds(pl.multiple_of(i * rb, rb), rb)
            wg[rws, :] = stage_g[ws, rws, :].astype(BF16)
            wu[rws, :] = stage_u[ws, rws, :].astype(BF16)
            wd[rws, :] = stage_d[ws, rws, :].astype(BF16)
            return carry

        lax.fori_loop(0, D_MODEL // rb, cast, 0)

    wait_rows(slot)
    xs = xin.at[slot]
    x = jnp.concatenate([xs[pl.ds(j, n_rows, stride=N_CHUNK), :] for j in range(N_CHUNK)], axis=1).astype(BF16)
    hid = (jax.nn.silu(_dot(x, wg[...])) * _dot(x, wu[...])).astype(BF16)
    y = _dot(hid, wd[...])
    s2 = rows + ROW_PAD
    for n in range(out_tiles):
        for j in range(N_CHUNK):
            o_ref[0, 0, n, pl.ds(j * s2, rows), :] = y[n * rows:(n + 1) * rows, j * LANES:(j + 1) * LANES]
            o_ref[0, 0, n, pl.ds(j * s2 + rows, ROW_PAD), :] = jnp.zeros((ROW_PAD, LANES), F32)

    for r in range(n_rows):
        row_copy(idx_nxt2, r, ahead).start(priority=r % 2)

    @pl.when(step == n_steps - 1)
    def _():
        wait_rows(ahead)
        wait_rows((step + 1) % ROW_SLOTS)


def _moe_ffn(rows_glob, h2_tt, w_gate, w_up, w_down, layer, out_tiles, rows):
    n_b = rows_glob.shape[0] // N_EXPERTS
    n_rows = out_tiles * rows
    n_steps = N_EXPERTS * n_b
    s2 = rows + ROW_PAD
    wshape = w_gate.shape[2:]
    smem = functools.partial(pl.BlockSpec, memory_space=pltpu.SMEM)
    hbm = pl.BlockSpec(memory_space=pl.ANY)
    return pl.pallas_call(
        functools.partial(_moe_kernel, layer=layer, n_b=n_b, out_tiles=out_tiles, rows=rows),
        grid=(N_EXPERTS, n_b),
        in_specs=[
            smem((1, 1, n_rows), lambda e, b: (e * n_b + b, 0, 0)),
            smem((1, 1, n_rows), lambda e, b: (jnp.minimum(e * n_b + b + 1, n_steps - 1), 0, 0)),
            smem((1, 1, n_rows), lambda e, b: (jnp.minimum(e * n_b + b + 2, n_steps - 1), 0, 0)),
            hbm, hbm, hbm, hbm,
        ],
        out_specs=pl.BlockSpec((1, 1, out_tiles, N_CHUNK * s2, LANES), lambda e, b: (b, e, 0, 0, 0)),
        out_shape=jax.ShapeDtypeStruct((n_b, N_EXPERTS, out_tiles, N_CHUNK * s2, LANES), F32),
        scratch_shapes=[
            pltpu.VMEM((ROW_SLOTS, n_rows * N_CHUNK, LANES), F32),
            pltpu.VMEM((2,) + wshape, F32), pltpu.VMEM((2,) + wshape, F32), pltpu.VMEM((2,) + wshape, F32),
            pltpu.VMEM(wshape, BF16), pltpu.VMEM(wshape, BF16), pltpu.VMEM(wshape, BF16),
            pltpu.SemaphoreType.DMA((ROW_SLOTS,)), pltpu.SemaphoreType.DMA((2,)),
        ],
        compiler_params=_cparams("arbitrary", "arbitrary"),
        name="expert_ffn",
    )(rows_glob, rows_glob, rows_glob, h2_tt, w_gate, w_up, w_down)


def _global_rows(idx, bsz, s, tiles):
    cap = idx.shape[-1]
    glob = idx.reshape(bsz, N_EXPERTS, cap) + (jnp.arange(bsz, dtype=jnp.int32) * s)[:, None, None]
    return _expert_major(glob, bsz, tiles)


def _expert_major(per_row, bsz, tiles):
    cap = per_row.shape[-1]
    return jnp.swapaxes(per_row.reshape(bsz, N_EXPERTS, cap), 0, 1).reshape(
        N_EXPERTS * (bsz // tiles), 1, tiles * cap)


COMBINE_UNROLL = 8
COMBINE_EXPERTS = 4


def _combine_kernel(idx_ref, gate_ref, x_ref, y_ref, mod_ref, o_ref, acc, *, s, sp, cap, s2, n_seg, to_rows):
    eg = pl.program_id(1)
    base = pl.program_id(0) * cap if n_seg > 1 else 0

    @pl.when(eg == 0)
    def _():
        for j in range(N_CHUNK):
            acc[pl.ds(j * sp, s), :] = x_ref[0, j]
            acc[pl.ds(j * sp + s, ROW_PAD), :] = jnp.zeros((ROW_PAD, LANES), F32)

    g2 = mod_ref[0, 5]

    for k in range(COMBINE_EXPERTS):
        y = y_ref.at[0, k, 0]

        def body(i, carry):
            new = []
            for u in range(COMBINE_UNROLL):
                r = i * COMBINE_UNROLL + u
                t = idx_ref[0, 0, k * cap + r]
                yrow = y[pl.ds(base + r, N_CHUNK, stride=s2), :]
                cur = acc[pl.ds(t, N_CHUNK, stride=sp), :]
                new.append((t, cur + (gate_ref[0, 0, k * cap + r] * g2) * yrow))
            for t, val in new:
                acc[pl.ds(t, N_CHUNK, stride=sp), :] = val
            return carry

        lax.fori_loop(0, cap // COMBINE_UNROLL, body, 0)

    @pl.when(eg == N_EXPERTS // COMBINE_EXPERTS - 1)
    def _():
        for j in range(N_CHUNK):
            if to_rows:
                o_ref[0, :, j * LANES:(j + 1) * LANES] = acc[pl.ds(j * sp, s), :]
            else:
                o_ref[0, j] = acc[pl.ds(j * sp, s), :]


def _combine(idx, gate, x1cm, ycm, mod, s, to_rows):
    bsz = x1cm.shape[0]
    sp = s + ROW_PAD
    cap = idx.shape[-1]
    n_b, _, tiles, rows8, _ = ycm.shape
    s2 = rows8 // N_CHUNK
    n_seg = bsz // (n_b * tiles)
    eg = COMBINE_EXPERTS
    y_spec = pl.BlockSpec((1, eg, 1, rows8, LANES), lambda b, g: (b // tiles // n_seg, g, b % tiles, 0, 0))
    smem = functools.partial(pl.BlockSpec, memory_space=pltpu.SMEM)
    cm_spec = pl.BlockSpec((1, N_CHUNK, s, LANES), lambda b, g: (b, 0, 0, 0))
    if to_rows:
        out_spec = pl.BlockSpec((1, s, D_MODEL), lambda b, g: (b, 0, 0))
        out_shape = jax.ShapeDtypeStruct((bsz, s, D_MODEL), F32)
    else:
        out_spec = cm_spec
        out_shape = jax.ShapeDtypeStruct((bsz, N_CHUNK, s, LANES), F32)
    n_grp = N_EXPERTS // eg
    return pl.pallas_call(
        functools.partial(_combine_kernel, s=s, sp=sp, cap=cap, s2=s2, n_seg=n_seg, to_rows=to_rows),
        grid=(bsz, n_grp),
        in_specs=[
            smem((1, 1, eg * cap), lambda b, g: (b * n_grp + g, 0, 0)),
            smem((1, 1, eg * cap), lambda b, g: (b * n_grp + g, 0, 0)),
            cm_spec,
            y_spec,
            pl.BlockSpec((1, 6, N_CHUNK, LANES), lambda b, g: (b, 0, 0, 0)),
        ],
        out_specs=out_spec,
        out_shape=out_shape,
        scratch_shapes=[pltpu.VMEM((N_CHUNK * sp, LANES), F32)],
        compiler_params=_cparams("parallel", "arbitrary"),
        name="combine_rows" if to_rows else "combine_cm",
    )(idx.reshape(bsz * n_grp, 1, eg * cap), gate.reshape(bsz * n_grp, 1, eg * cap),
      x1cm, ycm, mod.reshape(bsz, 6, N_CHUNK, LANES))


def _rope_tables(n):
    rows = n // GRID_W
    row = jnp.repeat(jnp.arange(rows, dtype=F32), GRID_W)
    col = jnp.tile(jnp.arange(GRID_W, dtype=F32), rows)
    n_freq = HEAD_DIM // 4
    inv = jnp.power(ROPE_THETA, -jnp.arange(n_freq, dtype=F32) / n_freq)
    ang = jnp.concatenate([row[:, None] * inv, col[:, None] * inv], axis=-1)
    cos, sin = jnp.cos(ang), jnp.sin(ang)
    return jnp.concatenate([cos, cos], axis=-1), jnp.concatenate([-sin, sin], axis=-1)


def _router_rows(router_w):
    wt = router_w.T
    hi = wt.astype(BF16)
    lo = (wt - hi.astype(F32)).astype(BF16)
    return jnp.concatenate([hi, lo], axis=0)


def kernel(x, c, ctx, c_ctx, mod_w, mod_b, norm1_g, norm2_g, router_w, exp_w_gate, exp_w_up, exp_w_down,
           g_w_in, g_v_norm_g, g_spatial_w, g_spatial_b, g_w_out, att_w_in, att_q_norm_g, att_k_norm_g,
           att_w_out, p_w_in, p_w_group, p_scale):
    bsz, s, d = x.shape
    s_ctx = ctx.shape[1]
    depth = mod_w.shape[0]
    assert d == D_MODEL and bsz * N_EXPERTS == LANES and bsz % MOE_TILES == 0
    assert s % GMLP_TILE == 0 and s_ctx % QKV_TILE == 0 and (s + s_ctx) % ATTN_KEY_BLOCK == 0

    cc = jnp.zeros((16, d), F32).at[:bsz].set(c).at[bsz].set(c_ctx)
    mods = _modulation(cc, mod_w, mod_b).reshape(depth, 16, 6, d)

    x_cur, x_cm = x, False
    c_cur, c_cm = ctx, False
    for i in range(depth):
        kind, j = i % 3, i // 3
        update_ctx = any(l % 3 == 1 for l in range(i + 1, depth))
        last = i == depth - 1
        mod_lat = mods[i, :bsz]
        mod_ctx = jnp.broadcast_to(mods[i, bsz][None], (bsz, 6, d))
        g1 = norm1_g[i][None]
        g2 = norm2_g[i][None]
        rw2 = _router_rows(router_w[i])

        ctx_out = None
        if kind == 0:
            s_bias = jnp.repeat(g_spatial_b[j].T, LANES, axis=1)
            gm = functools.partial(
                _gmlp_layer, g1=g1, w_in=g_w_in[j].astype(BF16), v_g=g_v_norm_g[j][None],
                s_w=g_spatial_w[j].astype(BF16), s_bias=s_bias, w_out=g_w_out[j].astype(BF16),
                g2=g2, rw2=rw2)
            x1, h2, lg = gm(x_cur, x_cm, s, mod_lat)
            if update_ctx:
                ctx_out = gm(c_cur, c_cm, s_ctx, mod_ctx)
        elif kind == 1:
            assert not update_ctx
            cos, sin = _rope_tables(s)
            w_in = att_w_in[j].astype(BF16)
            qg, kg = att_q_norm_g[j][None], att_k_norm_g[j][None]
            q, k, v = _qkv_proj(x_cur, x_cm, s, mod_lat, g1, w_in, qg, kg, cos, sin, True)
            kc, vc = _qkv_proj(c_cur, c_cm, s_ctx, mod_ctx, g1, w_in, qg, kg,
                               jnp.ones((s_ctx, HEAD_DIM), F32), jnp.zeros((s_ctx, HEAD_DIM), F32), False)
            k_all = jnp.concatenate([kc, k], axis=1)
            v_all = jnp.concatenate([vc, v], axis=1)
            assert x_cm
            x1, h2, lg = _attn_layer(x_cur, s, q, k_all, v_all, mod_lat, att_w_out[j].astype(BF16), g2, rw2)
        else:
            assert not update_ctx and x_cm
            x1, h2, lg = _pool_layer(x_cur, s, mod_lat, g1, p_w_in[j].astype(BF16),
                                     p_w_group[j].astype(BF16), p_scale[j][None], g2, rw2)

        experts = functools.partial(_moe_ffn, w_gate=exp_w_gate, w_up=exp_w_up, w_down=exp_w_down, layer=i)
        idx, gate = _route(lg, s)
        ys = experts(_global_rows(idx, bsz, s, MOE_TILES), h2.reshape(bsz * s * N_CHUNK, LANES),
                     out_tiles=MOE_TILES, rows=idx.shape[-1])
        x_cur = _combine(idx, gate, x1, ys, mod_lat, s, last)
        x_cm = True
        if ctx_out is not None:
            c1, hc2, lgc = ctx_out
            idx_c, gate_c = _route(lgc, s_ctx)
            ys_c = experts(_global_rows(idx_c, bsz, s_ctx, bsz), hc2.reshape(bsz * s_ctx * N_CHUNK, LANES),
                           out_tiles=1, rows=bsz * idx_c.shape[-1])
            c_cur = _combine(idx_c, gate_c, c1, ys_c, mod_ctx, s_ctx, False)
            c_cm = True
    return x_cur
```

```python
import functools

import jax
import jax.numpy as jnp
from jax import lax
from jax.experimental import pallas as pl
from jax.experimental.pallas import tpu as pltpu

F32 = jnp.float32
BF16 = jnp.bfloat16

D_MODEL = 1024
LANES = 128
SUBLANES = 8
N_CHUNK = D_MODEL // LANES
ROW_PAD = 8
EPS = 1e-6
GRID_W = 64
CHUNK = 128
A_GROUPS = 8
HEAD_DIM = 128
N_HEADS = 8
N_KV_HEADS = 2
Q_PER_KV = N_HEADS // N_KV_HEADS
ROPE_THETA = 10000.0
LOG2_E = 1.4426950408889634
GMLP_TILE = 1024
GMLP_SUB_TILE = 1024
QKV_TILE = 256
ATTN_Q_TILE = 512
ATTN_HEAD_GROUP = Q_PER_KV
ATTN_KEY_BLOCK = 768
POOL_TILE = 512
POOL_WINDOWS = (2, 4, 8, 16)
POOL_GROUP_DIM = D_MODEL // len(POOL_WINDOWS)
POOL_HALO = 8
N_EXPERTS = 16
CAPACITY_FACTOR = 2
VMEM_LIMIT = 56 * 1024 * 1024


def _cparams(*sem):
    return pltpu.CompilerParams(dimension_semantics=sem, vmem_limit_bytes=VMEM_LIMIT)


def _dot(a, b):
    return jnp.dot(a, b, preferred_element_type=F32)


def _dot_nt(a, b):
    return lax.dot_general(a, b, (((1,), (1,)), ((), ())), preferred_element_type=F32)


def _chunks_to_tile(ref, lead, rows):
    return jnp.concatenate([ref[lead + (j, rows)] for j in range(N_CHUNK)], axis=-1)


def _tile_to_chunks(ref, lead, rows, val):
    for j in range(N_CHUNK):
        ref[lead + (j, rows)] = val[:, j * LANES:(j + 1) * LANES]


def _rms_mod(x, g, shift, scale):
    y = x * lax.rsqrt(jnp.mean(x * x, axis=-1, keepdims=True) + EPS)
    return (y * g) * (1.0 + scale) + shift


def _mod_kernel(c_ref, w_ref, b_ref, o_ref):
    a = jax.nn.silu(c_ref[...]).astype(BF16)
    o_ref[0] = _dot(a, w_ref[0].astype(BF16)) + b_ref[0]


def _modulation(cc, mod_w, mod_b):
    depth, d, n = mod_w.shape
    tn = 1536
    return pl.pallas_call(
        _mod_kernel,
        grid=(depth, n // tn),
        in_specs=[
            pl.BlockSpec((16, d), lambda l, j: (0, 0)),
            pl.BlockSpec((1, d, tn), lambda l, j: (l, 0, j)),
            pl.BlockSpec((1, 1, tn), lambda l, j: (l, 0, j)),
        ],
        out_specs=pl.BlockSpec((1, 16, tn), lambda l, j: (l, 0, j)),
        out_shape=jax.ShapeDtypeStruct((depth, 16, n), F32),
        compiler_params=_cparams("parallel", "parallel"),
        name="adaln_modulation",
    )(cc, mod_w, mod_b.reshape(depth, 1, n))


def _load_x(x_ref, x_cm, r0=0, n=None):
    rows = slice(None) if n is None else slice(r0, r0 + n)
    if x_cm:
        return _chunks_to_tile(x_ref, (0,), rows)
    return x_ref[0, rows]


def _epilogue(x, y, mod_ref, g2_ref, rw_ref, x1_ref, h2_ref, lg_ref, r0=0):
    n = x.shape[0]
    x1 = x + mod_ref[0, 2:3, :] * y
    _tile_to_chunks(x1_ref, (0,), slice(r0, r0 + n), x1)
    h2 = _rms_mod(x1, g2_ref[...], mod_ref[0, 3:4, :], mod_ref[0, 4:5, :])
    for j in range(N_CHUNK):
        h2_ref[0, pl.ds(r0 * N_CHUNK + j, n, stride=N_CHUNK), :] = h2[:, j * LANES:(j + 1) * LANES]
    hi = h2.astype(BF16)
    lo = (h2 - hi.astype(F32)).astype(BF16)
    rw = rw_ref[...]
    a = _dot_nt(rw, hi)
    b = _dot_nt(rw[:N_EXPERTS], lo)
    lg_ref[0, :, r0:r0 + n] = a[:N_EXPERTS] + a[N_EXPERTS:] + b


def _x_spec(x_cm, tm, d):
    if x_cm:
        return pl.BlockSpec((1, N_CHUNK, tm, LANES), lambda b, i: (b, 0, i, 0))
    return pl.BlockSpec((1, tm, d), lambda b, i: (b, i, 0))


def _const_spec(shape):
    nd = len(shape)
    return pl.BlockSpec(shape, lambda b, i: (0,) * nd)


def _epilogue_specs(bsz, s, tm):
    cm_spec = pl.BlockSpec((1, N_CHUNK, tm, LANES), lambda b, i: (b, 0, i, 0))
    tt_spec = pl.BlockSpec((1, tm * N_CHUNK, LANES), lambda b, i: (b, i, 0))
    out_specs = [cm_spec, tt_spec, pl.BlockSpec((1, N_EXPERTS, tm), lambda b, i: (b, 0, i))]
    out_shape = [
        jax.ShapeDtypeStruct((bsz, N_CHUNK, s, LANES), F32),
        jax.ShapeDtypeStruct((bsz, s * N_CHUNK, LANES), F32),
        jax.ShapeDtypeStruct((bsz, N_EXPERTS, s), F32),
    ]
    return out_specs, out_shape


def _mod_spec():
    return pl.BlockSpec((1, 6, D_MODEL), lambda b, i: (b, 0, 0))


def _gmlp_kernel(x_ref, mod_ref, g1_ref, win_ref, vg_ref, sw_ref, sb_ref, wout_ref, g2_ref, rw_ref,
                 x1_ref, h2_ref, lg_ref, *, x_cm, tm):
    sub = min(tm, GMLP_SUB_TILE)
    nc = sub // CHUNK
    for r0 in range(0, tm, sub):
        x = _load_x(x_ref, x_cm, r0, sub)
        h = _rms_mod(x, g1_ref[...], mod_ref[0, 0:1, :], mod_ref[0, 1:2, :]).astype(BF16)
        z = jax.nn.gelu(_dot(h, win_ref[...]))
        u = z[:, :D_MODEL]
        v = z[:, D_MODEL:]
        mu = jnp.mean(v, axis=-1, keepdims=True)
        vc = v - mu
        v = (vc * lax.rsqrt(jnp.mean(vc * vc, axis=-1, keepdims=True) + EPS)) * vg_ref[...]
        vb = v.astype(BF16)
        mixed = []
        for g in range(A_GROUPS):
            cols = jnp.concatenate(
                [vb[c * CHUNK:(c + 1) * CHUNK, g * LANES:(g + 1) * LANES] for c in range(nc)], axis=1)
            mixed.append(_dot(sw_ref[g], cols))
        sv = jnp.concatenate(
            [jnp.concatenate([mixed[g][:, c * LANES:(c + 1) * LANES] for g in range(A_GROUPS)], axis=1)
             + sb_ref[...] for c in range(nc)], axis=0)
        y = _dot((u * sv).astype(BF16), wout_ref[...])
        _epilogue(x, y, mod_ref, g2_ref, rw_ref, x1_ref, h2_ref, lg_ref, r0)


def _gmlp_layer(x, x_cm, s, mod, g1, w_in, v_g, s_w, s_bias, w_out, g2, rw2):
    bsz = x.shape[0]
    tm = min(GMLP_TILE, s)
    out_specs, out_shape = _epilogue_specs(bsz, s, tm)
    return pl.pallas_call(
        functools.partial(_gmlp_kernel, x_cm=x_cm, tm=tm),
        grid=(bsz, s // tm),
        in_specs=[
            _x_spec(x_cm, tm, D_MODEL), _mod_spec(), _const_spec((1, D_MODEL)),
            _const_spec(w_in.shape), _const_spec((1, D_MODEL)), _const_spec(s_w.shape),
            _const_spec(s_bias.shape), _const_spec(w_out.shape), _const_spec((1, D_MODEL)),
            _const_spec(rw2.shape),
        ],
        out_specs=out_specs,
        out_shape=out_shape,
        compiler_params=_cparams("parallel", "arbitrary"),
        name="gmlp_mixer",
    )(x, mod, g1, w_in, v_g, s_w, s_bias, w_out, g2, rw2)


def _qkv_kernel(x_ref, mod_ref, g1_ref, w_ref, qg_ref, kg_ref, cos_ref, sin_ref, *out_refs,
                x_cm, want_q):
    x = _load_x(x_ref, x_cm)
    h = _rms_mod(x, g1_ref[...], mod_ref[0, 0:1, :], mod_ref[0, 1:2, :]).astype(BF16)
    qkv = _dot(h, w_ref[...])
    cos = cos_ref[...]
    sin = sin_ref[...]

    def norm_rope(t, g):
        t = (t * lax.rsqrt(jnp.mean(t * t, axis=-1, keepdims=True) + EPS)) * g
        return t * cos + pltpu.roll(t, HEAD_DIM // 2, 1) * sin

    qd = N_HEADS * HEAD_DIM
    kvd = N_KV_HEADS * HEAD_DIM
    if want_q:
        q_ref, k_ref, v_ref = out_refs
        for hd in range(N_HEADS):
            qh = norm_rope(qkv[:, hd * HEAD_DIM:(hd + 1) * HEAD_DIM], qg_ref[...])
            q_ref[0, :, hd * HEAD_DIM:(hd + 1) * HEAD_DIM] = (qh * (HEAD_DIM ** -0.5 * LOG2_E)).astype(BF16)
    else:
        k_ref, v_ref = out_refs
    for hd in range(N_KV_HEADS):
        kh = norm_rope(qkv[:, qd + hd * HEAD_DIM:qd + (hd + 1) * HEAD_DIM], kg_ref[...])
        k_ref[0, :, hd * HEAD_DIM:(hd + 1) * HEAD_DIM] = kh.astype(BF16)
    v_ref[0] = qkv[:, qd + kvd:].astype(BF16)


def _qkv_proj(x, x_cm, s, mod, g1, w_in, q_g, k_g, cos, sin, want_q):
    bsz = x.shape[0]
    tm = min(QKV_TILE, s)
    qd = N_HEADS * HEAD_DIM
    kvd = N_KV_HEADS * HEAD_DIM
    kv_spec = pl.BlockSpec((1, tm, kvd), lambda b, i: (b, i, 0))
    kv_shape = jax.ShapeDtypeStruct((bsz, s, kvd), BF16)
    out_specs = [kv_spec, kv_spec]
    out_shape = [kv_shape, kv_shape]
    if want_q:
        out_specs = [pl.BlockSpec((1, tm, qd), lambda b, i: (b, i, 0))] + out_specs
        out_shape = [jax.ShapeDtypeStruct((bsz, s, qd), BF16)] + out_shape
    return pl.pallas_call(
        functools.partial(_qkv_kernel, x_cm=x_cm, want_q=want_q),
        grid=(bsz, s // tm),
        in_specs=[
            _x_spec(x_cm, tm, D_MODEL), _mod_spec(), _const_spec((1, D_MODEL)),
            _const_spec(w_in.shape), _const_spec((1, HEAD_DIM)), _const_spec((1, HEAD_DIM)),
            pl.BlockSpec((tm, HEAD_DIM), lambda b, i: (i, 0)),
            pl.BlockSpec((tm, HEAD_DIM), lambda b, i: (i, 0)),
        ],
        out_specs=out_specs,
        out_shape=out_shape,
        compiler_params=_cparams("parallel", "arbitrary"),
        name="qkv_proj",
    )(x, mod, g1, w_in, q_g, k_g, cos, sin)


def _attn_kernel(x_ref, q_ref, k_ref, v_ref, mod_ref, wout_ref, g2_ref, rw_ref,
                 x1_ref, h2_ref, lg_ref, *, tq):
    x = _load_x(x_ref, True)
    heads = [None] * N_HEADS
    sk = k_ref.shape[1]
    ones_col = jnp.where(lax.broadcasted_iota(jnp.int32, (sk, HEAD_DIM), 1) == 0, 1.0, 0.0).astype(BF16)
    for h0 in range(0, N_HEADS, ATTN_HEAD_GROUP):
        kh = h0 // Q_PER_KV
        k = k_ref[0, :, kh * HEAD_DIM:(kh + 1) * HEAD_DIM]
        v = jnp.concatenate([v_ref[0, :, kh * HEAD_DIM:(kh + 1) * HEAD_DIM], ones_col], axis=1)
        qg = jnp.concatenate(
            [q_ref[0, :, (h0 + g) * HEAD_DIM:(h0 + g + 1) * HEAD_DIM] for g in range(ATTN_HEAD_GROUP)], axis=0)
        m = jnp.full((ATTN_HEAD_GROUP * tq, 1), -jnp.inf, F32)
        o = jnp.zeros((ATTN_HEAD_GROUP * tq, 2 * HEAD_DIM), F32)
        for kb in range(0, sk, ATTN_KEY_BLOCK):
            sc = _dot_nt(qg, k[kb:kb + ATTN_KEY_BLOCK])
            m_new = jnp.maximum(m, jnp.max(sc, axis=-1, keepdims=True))
            p = jnp.exp2(sc - m_new).astype(BF16)
            o = o * jnp.exp2(m - m_new) + _dot(p, v[kb:kb + ATTN_KEY_BLOCK])
            m = m_new
        o = o[:, :HEAD_DIM] / o[:, HEAD_DIM:HEAD_DIM + 1]
        for g in range(ATTN_HEAD_GROUP):
            heads[h0 + g] = o[g * tq:(g + 1) * tq].astype(BF16)
    y = _dot(jnp.concatenate(heads, axis=1), wout_ref[...])
    _epilogue(x, y, mod_ref, g2_ref, rw_ref, x1_ref, h2_ref, lg_ref)


def _attn_layer(x, s, q, k_all, v_all, mod, w_out, g2, rw2):
    bsz = x.shape[0]
    tq = ATTN_Q_TILE
    sk = k_all.shape[1]
    qd = N_HEADS * HEAD_DIM
    kvd = N_KV_HEADS * HEAD_DIM
    out_specs, out_shape = _epilogue_specs(bsz, s, tq)
    return pl.pallas_call(
        functools.partial(_attn_kernel, tq=tq),
        grid=(bsz, s // tq),
        in_specs=[
            _x_spec(True, tq, D_MODEL),
            pl.BlockSpec((1, tq, qd), lambda b, i: (b, i, 0)),
            pl.BlockSpec((1, sk, kvd), lambda b, i: (b, 0, 0)),
            pl.BlockSpec((1, sk, kvd), lambda b, i: (b, 0, 0)),
            _mod_spec(), _const_spec(w_out.shape), _const_spec((1, D_MODEL)), _const_spec(rw2.shape),
        ],
        out_specs=out_specs,
        out_shape=out_shape,
        compiler_params=_cparams("parallel", "arbitrary"),
        name="gqa_attention",
    )(x, q, k_all, v_all, mod, w_out, g2, rw2)


def _pool_kernel(xp_ref, x_ref, xn_ref, mod_ref, g1_ref, win_ref, band_ref, wgrp_ref, ps_ref, g2_ref,
                 rw_ref, x1_ref, h2_ref, lg_ref, *, tm, s):
    i = pl.program_id(1)
    x = _load_x(x_ref, True)
    xe = jnp.concatenate([_load_x(xp_ref, True), x, _load_x(xn_ref, True)], axis=0)
    h = _rms_mod(xe, g1_ref[...], mod_ref[0, 0:1, :], mod_ref[0, 1:2, :]).astype(BF16)
    z = _dot(h, win_ref[...])
    te = band_ref.shape[2]
    zp = z[:POOL_HALO] * jnp.where(i > 0, 1.0, 0.0)
    zn = z[POOL_HALO + tm:] * jnp.where(i < pl.num_programs(1) - 1, 1.0, 0.0)
    zc = z[POOL_HALO:POOL_HALO + tm]
    zb = jnp.concatenate([zp, zc, zn, jnp.zeros((te - tm - 2 * POOL_HALO, D_MODEL), F32)], axis=0).astype(BF16)
    tcol = i * tm + lax.broadcasted_iota(jnp.int32, (tm, 1), 0)
    outs = []
    for gi, w in enumerate(POOL_WINDOWS):
        half = w // 2
        cnt = (jnp.minimum(tcol + half - 1, s - 1) - jnp.maximum(tcol - half, 0) + 1).astype(F32)
        lo, hi_ = gi * POOL_GROUP_DIM, (gi + 1) * POOL_GROUP_DIM
        pooled = _dot(band_ref[gi], zb[:, lo:hi_]) / cnt - zc[:, lo:hi_]
        outs.append(_dot(pooled.astype(BF16), wgrp_ref[gi]))
    y = jnp.concatenate(outs, axis=1) * ps_ref[...]
    _epilogue(x, y, mod_ref, g2_ref, rw_ref, x1_ref, h2_ref, lg_ref)


def _pool_bands(tm):
    te = tm + LANES
    t = jnp.arange(tm)[:, None]
    off = jnp.arange(te)[None, :] - POOL_HALO - t
    inside = jnp.arange(te)[None, :] < tm + 2 * POOL_HALO
    return jnp.stack([(inside & (off >= -(w // 2)) & (off <= w // 2 - 1)) for w in POOL_WINDOWS]).astype(BF16)


def _pool_layer(x, s, mod, g1, w_in, w_grp, p_scale, g2, rw2):
    bsz = x.shape[0]
    tm = POOL_TILE
    nb = tm // POOL_HALO
    last = s // POOL_HALO - 1
    out_specs, out_shape = _epilogue_specs(bsz, s, tm)
    halo = (1, N_CHUNK, POOL_HALO, LANES)
    bands = _pool_bands(tm)
    return pl.pallas_call(
        functools.partial(_pool_kernel, tm=tm, s=s),
        grid=(bsz, s // tm),
        in_specs=[
            pl.BlockSpec(halo, lambda b, i: (b, 0, jnp.maximum(i * nb - 1, 0), 0)),
            _x_spec(True, tm, D_MODEL),
            pl.BlockSpec(halo, lambda b, i: (b, 0, jnp.minimum((i + 1) * nb, last), 0)),
            _mod_spec(), _const_spec((1, D_MODEL)), _const_spec(w_in.shape), _const_spec(bands.shape),
            _const_spec(w_grp.shape), _const_spec((1, D_MODEL)), _const_spec((1, D_MODEL)),
            _const_spec(rw2.shape),
        ],
        out_specs=out_specs,
        out_shape=out_shape,
        compiler_params=_cparams("parallel", "arbitrary"),
        name="pool_mixer",
    )(x, x, x, mod, g1, w_in, bands, w_grp, p_scale, g2, rw2)


SORT_PHASE_BITS = 4


def _comes_first(a, ia, b, ib):
    return (a > b) | ((a == b) & (ia < ib))


def _sort_tokens(key_ref, tok_ref, sub, n):
    nv = key_ref.shape[0]
    vbits = nv.bit_length() - 1

    def exchange(ka, ta, kb, tb, inv):
        keep = _comes_first(ka, ta, kb, tb)
        if inv is not None:
            keep = jnp.logical_xor(keep, inv)
        return jnp.where(keep, ka, kb), jnp.where(keep, ta, tb), jnp.where(keep, kb, ka), jnp.where(keep, tb, ta)

    k = 2
    while k <= n:
        dists = [k >> d for d in range(1, k.bit_length())]
        sub_inv = None if k >= n else ((sub & (k >> vbits)) != 0 if k >= nv else False)

        for j in [d for d in dists if d >= nv]:
            js = j >> vbits
            lower = (sub & js) == 0
            later = jnp.logical_not(lower) if sub_inv is None else jnp.logical_xor(lower, jnp.logical_not(sub_inv))

            def rotate_stage(g, carry, js=js, lower=lower, later=later):
                for u in range(SUBLANES):
                    v = g * SUBLANES + u
                    x, t = key_ref[v], tok_ref[v]
                    px = jnp.where(lower, pltpu.roll(x, SUBLANES - js, 0), pltpu.roll(x, js, 0))
                    pt = jnp.where(lower, pltpu.roll(t, SUBLANES - js, 0), pltpu.roll(t, js, 0))
                    keep = jnp.logical_xor(_comes_first(x, t, px, pt), later)
                    key_ref[v] = jnp.where(keep, x, px)
                    tok_ref[v] = jnp.where(keep, t, pt)
                return carry

            lax.fori_loop(0, nv // SUBLANES, rotate_stage, 0)

        vreg_dists = [d for d in dists if d < nv]
        for p0 in range(0, len(vreg_dists), SORT_PHASE_BITS):
            phase = vreg_dists[p0:p0 + SORT_PHASE_BITS]
            lo = phase[-1].bit_length() - 1
            nb = len(phase)
            members = [m << lo for m in range(1 << nb)]

            def group_phase(q, carry, phase=phase, lo=lo, nb=nb, members=members, k=k, sub_inv=sub_inv):
                base = ((q >> lo) << (lo + nb)) | (q & ((1 << lo) - 1))
                if sub_inv is False:
                    bit = (base >> (k.bit_length() - 1)) & 1
                    inv = lax.broadcast(bit, (SUBLANES, LANES)) != 0
                else:
                    inv = sub_inv
                ks = {m: key_ref[base + m] for m in members}
                ts = {m: tok_ref[base + m] for m in members}
                for j in phase:
                    for m in members:
                        if not m & j:
                            ks[m], ts[m], ks[m | j], ts[m | j] = exchange(ks[m], ts[m], ks[m | j], ts[m | j], inv)
                for m in members:
                    key_ref[base + m] = ks[m]
                    tok_ref[base + m] = ts[m]
                return carry

            lax.fori_loop(0, nv >> nb, group_phase, 0)
        k *= 2


def _route_kernel(lg_ref, idx_ref, gate_ref, key_ref, tok_ref, *, s, cap):
    lg = lg_ref[...]
    ex = jnp.exp(lg - jnp.max(lg, axis=1, keepdims=True))
    aff = ex / jnp.sum(ex, axis=1, keepdims=True)
    nv = s // SUBLANES
    key_ref[...] = jnp.transpose(aff.reshape(LANES, s)).reshape(nv, SUBLANES, LANES)
    tok_ref[...] = (lax.broadcasted_iota(jnp.int32, (nv, SUBLANES, LANES), 0) * SUBLANES
                    + lax.broadcasted_iota(jnp.int32, (nv, SUBLANES, LANES), 1))
    sub = lax.broadcasted_iota(jnp.int32, (SUBLANES, LANES), 0)
    _sort_tokens(key_ref, tok_ref, sub, s)
    top_k = key_ref[:, 0, :]
    top_i = tok_ref[:, 0, :].astype(F32)
    pad = LANES - cap % LANES if cap % LANES else 0
    if pad:
        top_k = jnp.concatenate([top_k, jnp.zeros((pad, LANES), F32)], axis=0)
        top_i = jnp.concatenate([top_i, jnp.zeros((pad, LANES), F32)], axis=0)
    gate_ref[...] = jnp.transpose(top_k)[:, :cap]
    idx_ref[...] = jnp.transpose(top_i)[:, :cap].astype(jnp.int32)


def _route(logits_t, s):
    bsz = logits_t.shape[0]
    cap = CAPACITY_FACTOR * s // N_EXPERTS
    n_rows = bsz * N_EXPERTS
    assert n_rows == LANES and cap == s // SUBLANES
    return pl.pallas_call(
        functools.partial(_route_kernel, s=s, cap=cap),
        out_shape=[
            jax.ShapeDtypeStruct((n_rows, cap), jnp.int32),
            jax.ShapeDtypeStruct((n_rows, cap), F32),
        ],
        scratch_shapes=[pltpu.VMEM((s // SUBLANES, SUBLANES, LANES), F32),
                        pltpu.VMEM((s // SUBLANES, SUBLANES, LANES), jnp.int32)],
        compiler_params=pltpu.CompilerParams(vmem_limit_bytes=VMEM_LIMIT),
        name="expert_choice_route",
    )(logits_t)


MOE_TILES = 4
ROW_SLOTS = 3


def _moe_kernel(idx_cur, idx_nxt, idx_nxt2, h2_hbm, wg_hbm, wu_hbm, wd_hbm, o_ref,
                xin, stage_g, stage_u, stage_d, wg, wu, wd, sem_x, sem_w,
                *, layer, n_b, out_tiles, rows):
    e = pl.program_id(0)
    b = pl.program_id(1)
    step = e * n_b + b
    n_steps = N_EXPERTS * n_b
    n_rows = out_tiles * rows
    slot = step % ROW_SLOTS
    ahead = (step + ROW_SLOTS - 1) % ROW_SLOTS

    def row_copy(idx_ref, r, dst_slot):
        src_row = pl.multiple_of(idx_ref[0, 0, r] * N_CHUNK, N_CHUNK)
        return pltpu.make_async_copy(h2_hbm.at[pl.ds(src_row, N_CHUNK), :],
                                     xin.at[dst_slot, pl.ds(r * N_CHUNK, N_CHUNK), :],
                                     sem_x.at[dst_slot])

    def wait_rows(s_):
        pltpu.make_async_copy(h2_hbm.at[pl.ds(0, n_rows * N_CHUNK), :], xin.at[s_], sem_x.at[s_]).wait()

    def weight_copies(expert):
        return [pltpu.make_async_copy(w.at[layer, expert], st, sem_w.at[0])
                for w, st in ((wg_hbm, stage_g), (wu_hbm, stage_u), (wd_hbm, stage_d))]

    @pl.when(step == 0)
    def _():
        for cp in weight_copies(0):
            cp.start()

        def issue(r, carry):
            row_copy(idx_cur, r, 0).start()
            row_copy(idx_nxt, r, 1).start()
            return carry

        lax.fori_loop(0, n_rows, issue, 0)

    @pl.when(b == 0)
    def _():
        for cp in weight_copies(e):
            cp.wait()

        rb = 128

        def cast(i, carry):
            rws = pl.ds(pl.multiple_of(i * rb, rb), rb)
            wg[rws, :] = stage_g[rws, :].astype(BF16)
            wu[rws, :] = stage_u[rws, :].astype(BF16)
            wd[rws, :] = stage_d[rws, :].astype(BF16)
            return carry

        lax.fori_loop(0, D_MODEL // rb, cast, 0)

        @pl.when(e + 1 < N_EXPERTS)
        def _():
            for cp in weight_copies(e + 1):
                cp.start()

    wait_rows(slot)
    xs = xin.at[slot]
    x = jnp.concatenate([xs[pl.ds(j, n_rows, stride=N_CHUNK), :] for j in range(N_CHUNK)], axis=1).astype(BF16)
    hid = (jax.nn.silu(_dot(x, wg[...])) * _dot(x, wu[...])).astype(BF16)
    y = _dot(hid, wd[...])
    s2 = rows + ROW_PAD
    for n in range(out_tiles):
        for j in range(N_CHUNK):
            o_ref[0, 0, n, pl.ds(j * s2, rows), :] = y[n * rows:(n + 1) * rows, j * LANES:(j + 1) * LANES]
            o_ref[0, 0, n, pl.ds(j * s2 + rows, ROW_PAD), :] = jnp.zeros((ROW_PAD, LANES), F32)

    for r in range(n_rows):
        row_copy(idx_nxt2, r, ahead).start(priority=r % 2)

    @pl.when(step == n_steps - 1)
    def _():
        wait_rows(ahead)
        wait_rows((step + 1) % ROW_SLOTS)


def _moe_ffn(rows_glob, h2_tt, w_gate, w_up, w_down, layer, out_tiles, rows):
    n_b = rows_glob.shape[0] // N_EXPERTS
    n_rows = out_tiles * rows
    n_steps = N_EXPERTS * n_b
    s2 = rows + ROW_PAD
    wshape = w_gate.shape[2:]
    smem = functools.partial(pl.BlockSpec, memory_space=pltpu.SMEM)
    hbm = pl.BlockSpec(memory_space=pl.ANY)
    return pl.pallas_call(
        functools.partial(_moe_kernel, layer=layer, n_b=n_b, out_tiles=out_tiles, rows=rows),
        grid=(N_EXPERTS, n_b),
        in_specs=[
            smem((1, 1, n_rows), lambda e, b: (e * n_b + b, 0, 0)),
            smem((1, 1, n_rows), lambda e, b: (jnp.minimum(e * n_b + b + 1, n_steps - 1), 0, 0)),
            smem((1, 1, n_rows), lambda e, b: (jnp.minimum(e * n_b + b + 2, n_steps - 1), 0, 0)),
            hbm, hbm, hbm, hbm,
        ],
        out_specs=pl.BlockSpec((1, 1, out_tiles, N_CHUNK * s2, LANES), lambda e, b: (b, e, 0, 0, 0)),
        out_shape=jax.ShapeDtypeStruct((n_b, N_EXPERTS, out_tiles, N_CHUNK * s2, LANES), F32),
        scratch_shapes=[
            pltpu.VMEM((ROW_SLOTS, n_rows * N_CHUNK, LANES), F32),
            pltpu.VMEM(wshape, F32), pltpu.VMEM(wshape, F32), pltpu.VMEM(wshape, F32),
            pltpu.VMEM(wshape, BF16), pltpu.VMEM(wshape, BF16), pltpu.VMEM(wshape, BF16),
            pltpu.SemaphoreType.DMA((ROW_SLOTS,)), pltpu.SemaphoreType.DMA((1,)),
        ],
        compiler_params=_cparams("arbitrary", "arbitrary"),
        name="expert_ffn",
    )(rows_glob, rows_glob, rows_glob, h2_tt, w_gate, w_up, w_down)


def _global_rows(idx, bsz, s, tiles):
    cap = idx.shape[-1]
    glob = idx.reshape(bsz, N_EXPERTS, cap) + (jnp.arange(bsz, dtype=jnp.int32) * s)[:, None, None]
    return _expert_major(glob, bsz, tiles)


def _expert_major(per_row, bsz, tiles):
    cap = per_row.shape[-1]
    return jnp.swapaxes(per_row.reshape(bsz, N_EXPERTS, cap), 0, 1).reshape(
        N_EXPERTS * (bsz // tiles), 1, tiles * cap)


COMBINE_UNROLL = 8
COMBINE_EXPERTS = 4


def _combine_kernel(idx_ref, gate_ref, x_ref, y_ref, mod_ref, o_ref, acc, *, s, sp, cap, s2, n_seg, to_rows):
    eg = pl.program_id(1)
    base = pl.program_id(0) * cap if n_seg > 1 else 0

    @pl.when(eg == 0)
    def _():
        for j in range(N_CHUNK):
            acc[pl.ds(j * sp, s), :] = x_ref[0, j]
            acc[pl.ds(j * sp + s, ROW_PAD), :] = jnp.zeros((ROW_PAD, LANES), F32)

    g2 = mod_ref[0, 5]

    for k in range(COMBINE_EXPERTS):
        y = y_ref.at[0, k, 0]

        def body(i, carry):
            new = []
            for u in range(COMBINE_UNROLL):
                r = i * COMBINE_UNROLL + u
                t = idx_ref[0, 0, k * cap + r]
                yrow = y[pl.ds(base + r, N_CHUNK, stride=s2), :]
                cur = acc[pl.ds(t, N_CHUNK, stride=sp), :]
                new.append((t, cur + (gate_ref[0, 0, k * cap + r] * g2) * yrow))
            for t, val in new:
                acc[pl.ds(t, N_CHUNK, stride=sp), :] = val
            return carry

        lax.fori_loop(0, cap // COMBINE_UNROLL, body, 0)

    @pl.when(eg == N_EXPERTS // COMBINE_EXPERTS - 1)
    def _():
        for j in range(N_CHUNK):
            if to_rows:
                o_ref[0, :, j * LANES:(j + 1) * LANES] = acc[pl.ds(j * sp, s), :]
            else:
                o_ref[0, j] = acc[pl.ds(j * sp, s), :]


def _combine(idx, gate, x1cm, ycm, mod, s, to_rows):
    bsz = x1cm.shape[0]
    sp = s + ROW_PAD
    cap = idx.shape[-1]
    n_b, _, tiles, rows8, _ = ycm.shape
    s2 = rows8 // N_CHUNK
    n_seg = bsz // (n_b * tiles)
    eg = COMBINE_EXPERTS
    y_spec = pl.BlockSpec((1, eg, 1, rows8, LANES), lambda b, g: (b // tiles // n_seg, g, b % tiles, 0, 0))
    smem = functools.partial(pl.BlockSpec, memory_space=pltpu.SMEM)
    cm_spec = pl.BlockSpec((1, N_CHUNK, s, LANES), lambda b, g: (b, 0, 0, 0))
    if to_rows:
        out_spec = pl.BlockSpec((1, s, D_MODEL), lambda b, g: (b, 0, 0))
        out_shape = jax.ShapeDtypeStruct((bsz, s, D_MODEL), F32)
    else:
        out_spec = cm_spec
        out_shape = jax.ShapeDtypeStruct((bsz, N_CHUNK, s, LANES), F32)
    n_grp = N_EXPERTS // eg
    return pl.pallas_call(
        functools.partial(_combine_kernel, s=s, sp=sp, cap=cap, s2=s2, n_seg=n_seg, to_rows=to_rows),
        grid=(bsz, n_grp),
        in_specs=[
            smem((1, 1, eg * cap), lambda b, g: (b * n_grp + g, 0, 0)),
            smem((1, 1, eg * cap), lambda b, g: (b * n_grp + g, 0, 0)),
            cm_spec,
            y_spec,
            pl.BlockSpec((1, 6, N_CHUNK, LANES), lambda b, g: (b, 0, 0, 0)),
        ],
        out_specs=out_spec,
        out_shape=out_shape,
        scratch_shapes=[pltpu.VMEM((N_CHUNK * sp, LANES), F32)],
        compiler_params=_cparams("parallel", "arbitrary"),
        name="combine_rows" if to_rows else "combine_cm",
    )(idx.reshape(bsz * n_grp, 1, eg * cap), gate.reshape(bsz * n_grp, 1, eg * cap),
      x1cm, ycm, mod.reshape(bsz, 6, N_CHUNK, LANES))


def _rope_tables(n):
    rows = n // GRID_W
    row = jnp.repeat(jnp.arange(rows, dtype=F32), GRID_W)
    col = jnp.tile(jnp.arange(GRID_W, dtype=F32), rows)
    n_freq = HEAD_DIM // 4
    inv = jnp.power(ROPE_THETA, -jnp.arange(n_freq, dtype=F32) / n_freq)
    ang = jnp.concatenate([row[:, None] * inv, col[:, None] * inv], axis=-1)
    cos, sin = jnp.cos(ang), jnp.sin(ang)
    return jnp.concatenate([cos, cos], axis=-1), jnp.concatenate([-sin, sin], axis=-1)


def _router_rows(router_w):
    wt = router_w.T
    hi = wt.astype(BF16)
    lo = (wt - hi.astype(F32)).astype(BF16)
    return jnp.concatenate([hi, lo], axis=0)


def kernel(x, c, ctx, c_ctx, mod_w, mod_b, norm1_g, norm2_g, router_w, exp_w_gate, exp_w_up, exp_w_down,
           g_w_in, g_v_norm_g, g_spatial_w, g_spatial_b, g_w_out, att_w_in, att_q_norm_g, att_k_norm_g,
           att_w_out, p_w_in, p_w_group, p_scale):
    bsz, s, d = x.shape
    s_ctx = ctx.shape[1]
    depth = mod_w.shape[0]
    assert d == D_MODEL and bsz * N_EXPERTS == LANES and bsz % MOE_TILES == 0
    assert s % GMLP_TILE == 0 and s_ctx % QKV_TILE == 0 and (s + s_ctx) % ATTN_KEY_BLOCK == 0

    cc = jnp.zeros((16, d), F32).at[:bsz].set(c).at[bsz].set(c_ctx)
    mods = _modulation(cc, mod_w, mod_b).reshape(depth, 16, 6, d)

    x_cur, x_cm = x, False
    c_cur, c_cm = ctx, False
    for i in range(depth):
        kind, j = i % 3, i // 3
        update_ctx = any(l % 3 == 1 for l in range(i + 1, depth))
        last = i == depth - 1
        mod_lat = mods[i, :bsz]
        mod_ctx = jnp.broadcast_to(mods[i, bsz][None], (bsz, 6, d))
        g1 = norm1_g[i][None]
        g2 = norm2_g[i][None]
        rw2 = _router_rows(router_w[i])

        ctx_out = None
        if kind == 0:
            s_bias = jnp.repeat(g_spatial_b[j].T, LANES, axis=1)
            gm = functools.partial(
                _gmlp_layer, g1=g1, w_in=g_w_in[j].astype(BF16), v_g=g_v_norm_g[j][None],
                s_w=g_spatial_w[j].astype(BF16), s_bias=s_bias, w_out=g_w_out[j].astype(BF16),
                g2=g2, rw2=rw2)
            x1, h2, lg = gm(x_cur, x_cm, s, mod_lat)
            if update_ctx:
                ctx_out = gm(c_cur, c_cm, s_ctx, mod_ctx)
        elif kind == 1:
            assert not update_ctx
            cos, sin = _rope_tables(s)
            w_in = att_w_in[j].astype(BF16)
            qg, kg = att_q_norm_g[j][None], att_k_norm_g[j][None]
            q, k, v = _qkv_proj(x_cur, x_cm, s, mod_lat, g1, w_in, qg, kg, cos, sin, True)
            kc, vc = _qkv_proj(c_cur, c_cm, s_ctx, mod_ctx, g1, w_in, qg, kg,
                               jnp.ones((s_ctx, HEAD_DIM), F32), jnp.zeros((s_ctx, HEAD_DIM), F32), False)
            k_all = jnp.concatenate([kc, k], axis=1)
            v_all = jnp.concatenate([vc, v], axis=1)
            assert x_cm
            x1, h2, lg = _attn_layer(x_cur, s, q, k_all, v_all, mod_lat, att_w_out[j].astype(BF16), g2, rw2)
        else:
            assert not update_ctx and x_cm
            x1, h2, lg = _pool_layer(x_cur, s, mod_lat, g1, p_w_in[j].astype(BF16),
                                     p_w_group[j].astype(BF16), p_scale[j][None], g2, rw2)

        experts = functools.partial(_moe_ffn, w_gate=exp_w_gate, w_up=exp_w_up, w_down=exp_w_down, layer=i)
        idx, gate = _route(lg, s)
        ys = experts(_global_rows(idx, bsz, s, MOE_TILES), h2.reshape(bsz * s * N_CHUNK, LANES),
                     out_tiles=MOE_TILES, rows=idx.shape[-1])
        x_cur = _combine(idx, gate, x1, ys, mod_lat, s, last)
        x_cm = True
        if ctx_out is not None:
            c1, hc2, lgc = ctx_out
            idx_c, gate_c = _route(lgc, s_ctx)
            ys_c = experts(_global_rows(idx_c, bsz, s_ctx, bsz), hc2.reshape(bsz * s_ctx * N_CHUNK, LANES),
                           out_tiles=1, rows=bsz * idx_c.shape[-1])
            c_cur = _combine(idx_c, gate_c, c1, ys_c, mod_ctx, s_ctx, False)
            c_cm = True
    return x_cur
```
